```python
import jax, jax.numpy as jnp
from jax import lax
import numpy as np

D_MODEL = 2048
BATCH = 4
SEQ = 8192
DEPTH = 1
DEC_BATCH = 2
DEC_SEQ = 4096
PAST_LEN = 128

D_MLSTM = 1024
N_MLSTM_HEADS = 4
HEAD_DIM = D_MLSTM // N_MLSTM_HEADS
D_POOL = D_MODEL - D_MLSTM
POOL_WINDOWS = (2, 4, 8, 16)
N_POOL_GROUPS = len(POOL_WINDOWS)
POOL_GROUP_DIM = D_POOL // N_POOL_GROUPS
N_GATES = 4 * N_MLSTM_HEADS
D_IN = 4 * D_MLSTM + N_GATES + D_POOL
D_FF = 5632
CONV_WIDTH = 3
CHUNK = 128
LN_EPS = 1e-5
DEEPNORM_ALPHA = (2.0 * DEPTH) ** 0.25
DEEPNORM_BETA = (8.0 * DEPTH) ** -0.25

kernel_name = "hymba_mlstm_multipool_convffn_encoder"


def layer_norm(x, g, b):
    xf = x.astype(jnp.float32)
    mu = jnp.mean(xf, axis=-1, keepdims=True)
    var = jnp.mean(jnp.square(xf - mu), axis=-1, keepdims=True)
    y = (xf - mu) * lax.rsqrt(var + LN_EPS) * g.astype(jnp.float32) + b.astype(jnp.float32)
    return y.astype(x.dtype)


def _mlstm_chunk_step(carry, xs):
    C, n, m = carry
    q, k, v, ig, lf = xs
    L = q.shape[2]
    b = jnp.cumsum(lf, axis=-1)
    lower = jnp.tril(jnp.ones((L, L), dtype=bool))
    D = jnp.where(lower, b[..., :, None] - b[..., None, :] + ig[..., None, :], -jnp.inf)
    inter = b + m[..., None]
    m_t = jnp.maximum(inter, jnp.max(D, axis=-1))
    w_intra = jnp.exp(D - m_t[..., None])
    w_inter = jnp.exp(inter - m_t)
    s = jnp.einsum('bhtd,bhsd->bhts', q, k) * w_intra
    num = jnp.einsum('bhts,bhsd->bhtd', s, v) + w_inter[..., None] * jnp.einsum('bhtd,bhde->bhte', q, C)
    den = jnp.sum(s, axis=-1) + w_inter * jnp.einsum('bhtd,bhd->bht', q, n)
    h = num / jnp.maximum(jnp.abs(den), jnp.exp(-m_t))[..., None]
    bL = b[..., -1]
    g = bL[..., None] - b + ig
    m_new = jnp.maximum(bL + m, jnp.max(g, axis=-1))
    decay = jnp.exp(bL + m - m_new)
    w_s = jnp.exp(g - m_new[..., None])
    C_new = decay[..., None, None] * C + jnp.einsum('bhs,bhsd,bhse->bhde', w_s, k, v)
    n_new = decay[..., None] * n + jnp.einsum('bhs,bhsd->bhd', w_s, k)
    return (C_new, n_new, m_new), h


def mlstm_scan(q, k, v, ig, lf):
    B, H, S, d = q.shape
    nc = S // CHUNK
    to_chunks = lambda a: jnp.moveaxis(a.reshape((B, H, nc, CHUNK) + a.shape[3:]), 2, 0)
    init = (jnp.zeros((B, H, d, d), jnp.float32), jnp.zeros((B, H, d), jnp.float32),
            jnp.zeros((B, H), jnp.float32))
    _, h = lax.scan(_mlstm_chunk_step, init, tuple(to_chunks(a) for a in (q, k, v, ig, lf)))
    return jnp.moveaxis(h, 0, 2).reshape(B, H, S, d)


def centred_mean_minus_self(u, w):
    S = u.shape[1]
    P = jnp.concatenate([jnp.zeros_like(u[:, :1]), jnp.cumsum(u, axis=1)], axis=1)
    t = jnp.arange(S)
    lo = jnp.clip(t - w // 2, 0, S)
    hi = jnp.clip(t + (w - w // 2), 0, S)
    cnt = (hi - lo).astype(jnp.float32)
    mean = (jnp.take(P, hi, axis=1) - jnp.take(P, lo, axis=1)) / cnt[None, :, None]
    return mean - u


def encoder_layer(x, w_in, b_in, mh_norm_g, w_pool, pool_scale, w_out, b_out, ln1_g, ln1_b,
                  w_up, b_up, w_conv, b_conv, w_down, b_down, ln2_g, ln2_b):
    B, S, _ = x.shape
    H, hd = N_MLSTM_HEADS, HEAD_DIM
    z = x @ w_in + b_in
    q, k, v, o, gates, u = jnp.split(
        z, np.cumsum([D_MLSTM, D_MLSTM, D_MLSTM, D_MLSTM, N_GATES]).tolist(), axis=-1)
    heads = lambda a: a.astype(jnp.float32).reshape(B, S, H, hd).transpose(0, 2, 1, 3)
    qh, kh, vh = heads(q), heads(k) * (hd ** -0.5), heads(v)
    gates = gates.astype(jnp.float32).transpose(0, 2, 1)
    ig_f, fp_f, ig_b, fp_b = jnp.split(gates, 4, axis=1)
    h_fwd = mlstm_scan(qh, kh, vh, ig_f, jax.nn.log_sigmoid(fp_f))
    fl = lambda a: jnp.flip(a, axis=2)
    h_bwd = fl(mlstm_scan(fl(qh), fl(kh), fl(vh), fl(ig_b), fl(jax.nn.log_sigmoid(fp_b))))
    hm = (h_fwd + h_bwd).transpose(0, 2, 1, 3)
    mu = jnp.mean(hm, axis=-1, keepdims=True)
    var = jnp.mean(jnp.square(hm - mu), axis=-1, keepdims=True)
    hm = ((hm - mu) * lax.rsqrt(var + LN_EPS)).reshape(B, S, D_MLSTM) * mh_norm_g.astype(jnp.float32)
    hm = hm * jax.nn.sigmoid(o.astype(jnp.float32))
    ug = u.astype(jnp.float32).reshape(B, S, N_POOL_GROUPS, POOL_GROUP_DIM)
    pooled = jnp.stack([centred_mean_minus_self(ug[:, :, gi], w) for gi, w in enumerate(POOL_WINDOWS)], axis=2)
    hp = jnp.einsum('bsgc,gcd->bsgd', pooled, w_pool.astype(jnp.float32)).reshape(B, S, D_POOL)
    hp = hp * pool_scale.astype(jnp.float32)
    mix = jnp.concatenate([hm, hp], axis=-1).astype(x.dtype) @ w_out + b_out
    x = layer_norm(DEEPNORM_ALPHA * x + mix, ln1_g, ln1_b)
    a = x @ w_up + b_up
    ap = jnp.pad(a, ((0, 0), (1, 1), (0, 0)))
    a = w_conv[0] * ap[:, :-2] + w_conv[1] * ap[:, 1:-1] + w_conv[2] * ap[:, 2:] + b_conv
    gate, val = jnp.split(a, 2, axis=-1)
    f = (jax.nn.silu(gate) * val) @ w_down + b_down
    return layer_norm(DEEPNORM_ALPHA * x + f, ln2_g, ln2_b)


def run_trunk(x, w_in, b_in, mh_norm_g, w_pool, pool_scale, w_out, b_out, ln1_g, ln1_b,
              w_up, b_up, w_conv, b_conv, w_down, b_down, ln2_g, ln2_b):
    for l in range(DEPTH):
        x = encoder_layer(x, w_in[l], b_in[l], mh_norm_g[l], w_pool[l], pool_scale[l], w_out[l], b_out[l],
                          ln1_g[l], ln1_b[l], w_up[l], b_up[l], w_conv[l], b_conv[l], w_down[l], b_down[l],
                          ln2_g[l], ln2_b[l])
    return x


def setup_inputs(seed: int = 0) -> dict:
    key = jax.random.key(seed)
    ks = jax.random.split(key, 20)
    nrm = lambda k, shape, s: jax.random.normal(k, shape, jnp.float32) * s
    gain = lambda k, n: 1.0 + nrm(k, (DEPTH, n), 0.02)
    H = N_MLSTM_HEADS
    b_in = nrm(ks[3], (DEPTH, D_IN), 0.02)
    f_bias = jnp.linspace(3.0, 6.0, H, dtype=jnp.float32)
    g0 = 4 * D_MLSTM
    b_in = b_in.at[:, g0 + H:g0 + 2 * H].add(f_bias).at[:, g0 + 3 * H:g0 + 4 * H].add(f_bias)
    return {
        "x_prompt": nrm(ks[0], (BATCH, SEQ, D_MODEL), 1.0),
        "x_sample": nrm(ks[1], (DEC_BATCH, DEC_SEQ, D_MODEL), 1.0),
        "w_in": nrm(ks[2], (DEPTH, D_MODEL, D_IN), D_MODEL ** -0.5),
        "b_in": b_in,
        "mh_norm_g": gain(ks[4], D_MLSTM),
        "w_pool": nrm(ks[5], (DEPTH, N_POOL_GROUPS, POOL_GROUP_DIM, POOL_GROUP_DIM), POOL_GROUP_DIM ** -0.5),
        "pool_scale": gain(ks[6], D_POOL),
        "w_out": nrm(ks[7], (DEPTH, D_MODEL, D_MODEL), DEEPNORM_BETA * D_MODEL ** -0.5),
        "b_out": nrm(ks[8], (DEPTH, D_MODEL), 0.02),
        "ln1_g": gain(ks[9], D_MODEL),
        "ln1_b": nrm(ks[10], (DEPTH, D_MODEL), 0.02),
        "w_up": nrm(ks[11], (DEPTH, D_MODEL, 2 * D_FF), D_MODEL ** -0.5),
        "b_up": nrm(ks[12], (DEPTH, 2 * D_FF), 0.02),
        "w_conv": nrm(ks[13], (DEPTH, CONV_WIDTH, 2 * D_FF), CONV_WIDTH ** -0.5),
        "b_conv": nrm(ks[14], (DEPTH, 2 * D_FF), 0.02),
        "w_down": nrm(ks[15], (DEPTH, D_FF, D_MODEL), DEEPNORM_BETA * D_FF ** -0.5),
        "b_down": nrm(ks[16], (DEPTH, D_MODEL), 0.02),
        "ln2_g": gain(ks[17], D_MODEL),
        "ln2_b": nrm(ks[18], (DEPTH, D_MODEL), 0.02),
    }


def reference(x_prompt, x_sample, w_in, b_in, mh_norm_g, w_pool, pool_scale, w_out, b_out, ln1_g, ln1_b,
              w_up, b_up, w_conv, b_conv, w_down, b_down, ln2_g, ln2_b):
    y_prompt = run_trunk(x_prompt, w_in, b_in, mh_norm_g, w_pool, pool_scale, w_out, b_out, ln1_g, ln1_b,
                         w_up, b_up, w_conv, b_conv, w_down, b_down, ln2_g, ln2_b)
    y_sample = run_trunk(x_sample, w_in, b_in, mh_norm_g, w_pool, pool_scale, w_out, b_out, ln1_g, ln1_b,
                         w_up, b_up, w_conv, b_conv, w_down, b_down, ln2_g, ln2_b)
    return (y_prompt, y_sample)
```

```python
import functools

import jax
import jax.numpy as jnp
from jax import lax
from jax.experimental import pallas as pl
from jax.experimental.pallas import tpu as pltpu

D_MODEL = 2048
D_MLSTM = 1024
N_HEADS = 4
HEAD_DIM = D_MLSTM // N_HEADS
D_POOL = D_MODEL - D_MLSTM
POOL_WINDOWS = (2, 4, 8, 16)
POOL_GROUP_DIM = D_POOL // len(POOL_WINDOWS)
D_FF = 5632
LN_EPS = 1e-5

LANES = 128
SUBLANES = 8
HALO = SUBLANES
CHUNK = 128
GATE_COLS = 2 * LANES
D_IN_PACKED = 4 * D_MLSTM + D_POOL + GATE_COLS
VMEM_LIMIT = 56 * 1024 * 1024

MXU_DTYPE = jnp.bfloat16
F32 = jnp.float32


def _mm(a, b):
    return jnp.dot(a, b, preferred_element_type=F32)


def _layer_norm_rows(y, g, b):
    mu = jnp.mean(y, axis=-1, keepdims=True)
    d = y - mu
    var = jnp.mean(d * d, axis=-1, keepdims=True)
    return d * lax.rsqrt(var + LN_EPS) * g + b


def _const_spec(shape):
    nd = len(shape)
    return pl.BlockSpec(shape, lambda *_: (0,) * nd, pipeline_mode=pl.Buffered(1))


def _inproj_kernel(x_ref, w_ref, b_ref, q_ref, k_ref, v_ref, o_ref, u_ref, gi_ref, gf_ref):
    xb = x_ref[...].astype(MXU_DTYPE)

    def proj(c0, c1):
        return _mm(xb, w_ref[:, c0:c1]) + b_ref[:, c0:c1]

    d = D_MLSTM
    q_ref[...] = proj(0, d).astype(q_ref.dtype)
    k_ref[...] = (proj(d, 2 * d) * (HEAD_DIM ** -0.5)).astype(k_ref.dtype)
    v_ref[...] = proj(2 * d, 3 * d).astype(v_ref.dtype)
    o_ref[...] = proj(3 * d, 4 * d)
    u_ref[...] = proj(4 * d, 4 * d + D_POOL)
    g = proj(4 * d + D_POOL, D_IN_PACKED)
    gi_ref[...] = g[:, :LANES]
    fp = g[:, LANES:]
    gf_ref[...] = jnp.minimum(fp, 0.0) - jnp.log1p(jnp.exp(-jnp.abs(fp)))


def _inproj(x2d, w_packed, b_packed, tm):
    n = x2d.shape[0]
    row = lambda w: pl.BlockSpec((tm, w), lambda i: (i, 0))
    out_shape = (
        jax.ShapeDtypeStruct((n, D_MLSTM), MXU_DTYPE),
        jax.ShapeDtypeStruct((n, D_MLSTM), MXU_DTYPE),
        jax.ShapeDtypeStruct((n, D_MLSTM), MXU_DTYPE),
        jax.ShapeDtypeStruct((n, D_MLSTM), F32),
        jax.ShapeDtypeStruct((n, D_POOL), F32),
        jax.ShapeDtypeStruct((n, LANES), F32),
        jax.ShapeDtypeStruct((n, LANES), F32),
    )
    return pl.pallas_call(
        _inproj_kernel,
        grid=(n // tm,),
        in_specs=[row(D_MODEL), _const_spec((D_MODEL, D_IN_PACKED)), _const_spec((1, D_IN_PACKED))],
        out_specs=(row(D_MLSTM), row(D_MLSTM), row(D_MLSTM), row(D_MLSTM), row(D_POOL),
                   row(LANES), row(LANES)),
        out_shape=out_shape,
        compiler_params=pltpu.CompilerParams(
            dimension_semantics=("parallel",), vmem_limit_bytes=VMEM_LIMIT),
        name="inproj",
    )(x2d, w_packed, b_packed)


def _split3(x):
    hi = x.astype(MXU_DTYPE)
    r1 = x - hi.astype(F32)
    mid = r1.astype(MXU_DTYPE)
    lo = (r1 - mid.astype(F32)).astype(MXU_DTYPE)
    return hi, mid, lo


def _cumsum_rows(tri, x):
    hi, mid, lo = _split3(x)
    return _mm(tri, hi) + _mm(tri, mid) + _mm(tri, lo)


def _cummax_rows(x, reverse):
    n = x.shape[0]
    ridx = lax.broadcasted_iota(jnp.int32, x.shape, 0)
    sh = 1
    while sh < n:
        if reverse:
            shifted = pltpu.roll(x, n - sh, axis=0)
            valid = ridx < n - sh
        else:
            shifted = pltpu.roll(x, sh, axis=0)
            valid = ridx >= sh
        x = jnp.maximum(x, jnp.where(valid, shifted, -jnp.inf))
        sh *= 2
    return x


def _mlstm_kernel(qf_ref, kf_ref, vf_ref, gif_ref, gff_ref,
                  qb_ref, kb_ref, vb_ref, gib_ref, gfb_ref,
                  hf_ref, hb_ref, c_ref, m_ref):
    L = CHUNK

    @pl.when(pl.program_id(1) == 0)
    def _():
        c_ref[...] = jnp.zeros_like(c_ref)
        m_ref[...] = jnp.zeros_like(m_ref)

    t_idx = lax.broadcasted_iota(jnp.int32, (L, L), 0)
    s_idx = lax.broadcasted_iota(jnp.int32, (L, L), 1)
    m_prev = m_ref[0:1, :]
    ones_ext = jnp.ones((L, LANES), MXU_DTYPE)
    m_new_rows = []

    for direction, (q_ref, k_ref, v_ref, gi_ref, gf_ref, h_ref) in enumerate((
            (qf_ref, kf_ref, vf_ref, gif_ref, gff_ref, hf_ref),
            (qb_ref, kb_ref, vb_ref, gib_ref, gfb_ref, hb_ref))):
        reverse = direction == 1
        mask = (s_idx >= t_idx) if reverse else (s_idx <= t_idx)
        last = 0 if reverse else L - 1
        b = _cumsum_rows(mask.astype(MXU_DTYPE), gf_ref[...])
        a = gi_ref[...] - b
        big_m = jnp.maximum(_cummax_rows(a, reverse), m_prev)
        w_inter = jnp.exp(m_prev - big_m)
        exp_neg_m = jnp.exp(-(b + big_m))
        m_last = big_m[last:last + 1, :]
        decay = jnp.exp(m_prev - m_last)
        w_state = jnp.exp(a - m_last)
        m_new_rows.append(b[last:last + 1, :] + m_last)
        a_t = a.T

        for h in range(N_HEADS):
            lane = direction * N_HEADS + h
            cols = slice(h * HEAD_DIM, (h + 1) * HEAD_DIM)
            q = q_ref[:, cols]
            k = k_ref[:, cols]
            v_ext = jnp.concatenate([v_ref[:, cols], ones_ext], axis=1)
            s_qk = lax.dot_general(q, k, (((1,), (1,)), ((), ())), preferred_element_type=F32)
            logw = jnp.where(mask, a_t[lane:lane + 1, :] - big_m[:, lane:lane + 1], -jnp.inf)
            s = (s_qk * jnp.exp(logw)).astype(MXU_DTYPE)
            c_old = c_ref[lane]
            r = _mm(s, v_ext) + w_inter[:, lane:lane + 1] * _mm(q, c_old.astype(MXU_DTYPE))
            den = jnp.maximum(jnp.abs(r[:, HEAD_DIM:]), exp_neg_m[:, lane:lane + 1])
            inv = 1.0 / den
            h_ref[:, cols] = r[:, :HEAD_DIM] * jnp.concatenate([inv] * (HEAD_DIM // LANES), axis=1)
            kw = (k.astype(F32) * w_state[:, lane:lane + 1]).astype(MXU_DTYPE)
            upd = lax.dot_general(kw, v_ext, (((0,), (0,)), ((), ())), preferred_element_type=F32)
            c_ref[lane] = decay[:, lane:lane + 1] * c_old + upd

    lane_idx = lax.broadcasted_iota(jnp.int32, (1, LANES), 1)
    m_ref[0:1, :] = jnp.where(lane_idx < N_HEADS, m_new_rows[0], m_new_rows[1])


def _mlstm(q, k, v, gi, gf, batch, seq):
    n = q.shape[0]
    nc = seq // CHUNK
    fwd = lambda w: pl.BlockSpec((CHUNK, w), lambda b, c: (b * nc + c, 0))
    bwd = lambda w: pl.BlockSpec((CHUNK, w), lambda b, c: (b * nc + nc - 1 - c, 0))
    return pl.pallas_call(
        _mlstm_kernel,
        grid=(batch, nc),
        in_specs=[fwd(D_MLSTM), fwd(D_MLSTM), fwd(D_MLSTM), fwd(LANES), fwd(LANES),
                  bwd(D_MLSTM), bwd(D_MLSTM), bwd(D_MLSTM), bwd(LANES), bwd(LANES)],
        out_specs=(fwd(D_MLSTM), bwd(D_MLSTM)),
        out_shape=(jax.ShapeDtypeStruct((n, D_MLSTM), F32), jax.ShapeDtypeStruct((n, D_MLSTM), F32)),
        scratch_shapes=[pltpu.VMEM((2 * N_HEADS, HEAD_DIM, HEAD_DIM + LANES), F32),
                        pltpu.VMEM((SUBLANES, LANES), F32)],
        compiler_params=pltpu.CompilerParams(
            dimension_semantics=("parallel", "arbitrary"), vmem_limit_bytes=VMEM_LIMIT),
        name="mlstm_scan",
    )(q, k, v, gi, gf, q, k, v, gi, gf)


def _outproj_kernel(hf_ref, hb_ref, o_ref, u_ref, up_ref, un_ref, x_ref,
                    mhg_ref, wp_ref, ps_ref, wo_ref, bo_ref, g_ref, b_ref,
                    y_ref, uext_ref, *, tiles_per_seq, seq, alpha):
    tm = x_ref.shape[0]
    t_in_seq = pl.program_id(0) % tiles_per_seq
    not_first = (t_in_seq > 0).astype(F32)
    not_last = (t_in_seq < tiles_per_seq - 1).astype(F32)

    hs = hf_ref[...] + hb_ref[...]
    parts = []
    for h in range(N_HEADS):
        xh = hs[:, h * HEAD_DIM:(h + 1) * HEAD_DIM]
        mu = jnp.mean(xh, axis=-1, keepdims=True)
        d = xh - mu
        var = jnp.mean(d * d, axis=-1, keepdims=True)
        parts.append(d * lax.rsqrt(var + LN_EPS))
    hm = jnp.concatenate(parts, axis=1) * mhg_ref[...] * jax.nn.sigmoid(o_ref[...])

    u = u_ref[...]
    uext_ref[0:HALO, :] = up_ref[...] * not_first
    uext_ref[HALO:HALO + tm, :] = u
    uext_ref[HALO + tm:, :] = un_ref[...] * not_last
    pos = t_in_seq * tm + lax.broadcasted_iota(jnp.int32, (tm, 1), 0)
    hp_parts = []
    for gi, w in enumerate(POOL_WINDOWS):
        cols = slice(gi * POOL_GROUP_DIM, (gi + 1) * POOL_GROUP_DIM)
        acc = uext_ref[HALO - w // 2:HALO - w // 2 + tm, cols]
        for j in range(-(w // 2) + 1, w - w // 2):
            acc = acc + uext_ref[HALO + j:HALO + j + tm, cols]
        cnt = (jnp.minimum(pos + (w - w // 2), seq) - jnp.maximum(pos - w // 2, 0)).astype(F32)
        pooled = acc / cnt - u[:, cols]
        hp_parts.append(_mm(pooled.astype(MXU_DTYPE), wp_ref[gi]))
    hp = jnp.concatenate(hp_parts, axis=1) * ps_ref[...]

    mix = _mm(jnp.concatenate([hm, hp], axis=1).astype(MXU_DTYPE), wo_ref[...]) + bo_ref[...]
    y_ref[...] = _layer_norm_rows(alpha * x_ref[...] + mix, g_ref[...], b_ref[...])


def _outproj(hf, hb, o, u, x2d, mh_g, w_pool, pool_scale, w_out, b_out, ln_g, ln_b, seq, tm, alpha):
    n = x2d.shape[0]
    tiles_per_seq = seq // tm
    hb_per_tile = tm // HALO
    n_halo_blocks = n // HALO
    row = lambda w: pl.BlockSpec((tm, w), lambda i: (i, 0))
    prev = pl.BlockSpec((HALO, D_POOL), lambda i: (jnp.maximum(i * hb_per_tile - 1, 0), 0))
    nxt = pl.BlockSpec((HALO, D_POOL),
                       lambda i: (jnp.minimum((i + 1) * hb_per_tile, n_halo_blocks - 1), 0))
    kern = functools.partial(_outproj_kernel, tiles_per_seq=tiles_per_seq, seq=seq, alpha=alpha)
    return pl.pallas_call(
        kern,
        grid=(n // tm,),
        in_specs=[row(D_MLSTM), row(D_MLSTM), row(D_MLSTM), row(D_POOL), prev, nxt, row(D_MODEL),
                  _const_spec((1, D_MLSTM)),
                  _const_spec((len(POOL_WINDOWS), POOL_GROUP_DIM, POOL_GROUP_DIM)),
                  _const_spec((1, D_POOL)), _const_spec((D_MODEL, D_MODEL)),
                  _const_spec((1, D_MODEL)), _const_spec((1, D_MODEL)), _const_spec((1, D_MODEL))],
        out_specs=row(D_MODEL),
        out_shape=jax.ShapeDtypeStruct((n, D_MODEL), F32),
        scratch_shapes=[pltpu.VMEM((tm + 2 * HALO, D_POOL), F32)],
        compiler_params=pltpu.CompilerParams(
            dimension_semantics=("parallel",), vmem_limit_bytes=VMEM_LIMIT),
        name="outproj_ln",
    )(hf, hb, o, u, u, u, x2d, mh_g, w_pool, pool_scale, w_out, b_out, ln_g, ln_b)


def _ffn_kernel(x_ref, xp_ref, xn_ref, wg_ref, wv_ref, bg_ref, bv_ref, cg_ref, cv_ref,
                bcg_ref, bcv_ref, wd_ref, bd_ref, g_ref, b_ref,
                y_ref, xs_ref, ag_ref, av_ref, *, tiles_per_seq, alpha):
    tm = x_ref.shape[0]
    j = pl.program_id(1)
    t_in_seq = pl.program_id(0) % tiles_per_seq

    @pl.when(j == 0)
    def _():
        x = x_ref[...]
        xs_ref[...] = jnp.concatenate([xp_ref[...], x, xn_ref[...]], axis=0).astype(xs_ref.dtype)
        y_ref[...] = alpha * x + bd_ref[...]

    xs = xs_ref[...]

    def conv_branch(w_ref, bias_ref, a_ref, cw_ref, cb_ref):
        a_ref[...] = _mm(xs, w_ref[...]) + bias_ref[...]

        @pl.when(t_in_seq == 0)
        def _():
            a_ref[HALO - 1:HALO, :] = jnp.zeros((1, a_ref.shape[1]), F32)

        @pl.when(t_in_seq == tiles_per_seq - 1)
        def _():
            a_ref[HALO + tm:HALO + tm + 1, :] = jnp.zeros((1, a_ref.shape[1]), F32)

        return (cw_ref[0:1, :] * a_ref[HALO - 1:HALO - 1 + tm, :]
                + cw_ref[1:2, :] * a_ref[HALO:HALO + tm, :]
                + cw_ref[2:3, :] * a_ref[HALO + 1:HALO + 1 + tm, :] + cb_ref[...])

    gate = conv_branch(wg_ref, bg_ref, ag_ref, cg_ref, bcg_ref)
    val = conv_branch(wv_ref, bv_ref, av_ref, cv_ref, bcv_ref)
    f = (gate * jax.nn.sigmoid(gate) * val).astype(MXU_DTYPE)
    y_ref[...] += _mm(f, wd_ref[...])

    @pl.when(j == pl.num_programs(1) - 1)
    def _():
        y_ref[...] = _layer_norm_rows(y_ref[...], g_ref[...], b_ref[...])


def _ffn(x1, w_gate, w_val, b_gate, b_val, c_gate, c_val, bc_gate, bc_val, w_down, b_down,
         ln_g, ln_b, seq, tm, tf, alpha):
    n = x1.shape[0]
    tiles_per_seq = seq // tm
    hb_per_tile = tm // HALO
    n_halo_blocks = n // HALO
    nj = D_FF // tf
    main = pl.BlockSpec((tm, D_MODEL), lambda i, j: (i, 0))
    prev = pl.BlockSpec((HALO, D_MODEL), lambda i, j: (jnp.maximum(i * hb_per_tile - 1, 0), 0))
    nxt = pl.BlockSpec((HALO, D_MODEL),
                       lambda i, j: (jnp.minimum((i + 1) * hb_per_tile, n_halo_blocks - 1), 0))
    up = pl.BlockSpec((D_MODEL, tf), lambda i, j: (0, j))
    ffrow = lambda r: pl.BlockSpec((r, tf), lambda i, j: (0, j))
    down = pl.BlockSpec((tf, D_MODEL), lambda i, j: (j, 0))
    vec = pl.BlockSpec((1, D_MODEL), lambda i, j: (0, 0))
    kern = functools.partial(_ffn_kernel, tiles_per_seq=tiles_per_seq, alpha=alpha)
    return pl.pallas_call(
        kern,
        grid=(n // tm, nj),
        in_specs=[main, prev, nxt, up, up, ffrow(1), ffrow(1), ffrow(3), ffrow(3),
                  ffrow(1), ffrow(1), down, vec, vec, vec],
        out_specs=main,
        out_shape=jax.ShapeDtypeStruct((n, D_MODEL), F32),
        scratch_shapes=[pltpu.VMEM((tm + 2 * HALO, D_MODEL), MXU_DTYPE),
                        pltpu.VMEM((tm + 2 * HALO, tf), F32),
                        pltpu.VMEM((tm + 2 * HALO, tf), F32)],
        compiler_params=pltpu.CompilerParams(
            dimension_semantics=("parallel", "arbitrary"), vmem_limit_bytes=VMEM_LIMIT),
        name="ffn_ln",
    )(x1, x1, x1, w_gate, w_val, b_gate, b_val, c_gate, c_val, bc_gate, bc_val, w_down, b_down,
      ln_g, ln_b)


def _pack_layer(w_in, b_in, mh_norm_g, w_pool, pool_scale, w_out, b_out, ln1_g, ln1_b,
                w_up, b_up, w_conv, b_conv, w_down, b_down, ln2_g, ln2_b):
    g0 = 4 * D_MLSTM
    h = N_HEADS
    ng = 4 * h

    def gate_cols(a):
        gts = a[..., g0:g0 + ng]
        pad = jnp.zeros(a.shape[:-1] + (LANES - 2 * h,), a.dtype)
        return jnp.concatenate([gts[..., 0:h], gts[..., 2 * h:3 * h], pad,
                                gts[..., h:2 * h], gts[..., 3 * h:4 * h], pad], axis=-1)

    pack_in = lambda a: jnp.concatenate([a[..., :g0], a[..., g0 + ng:], gate_cols(a)], axis=-1)
    row = lambda a: a.reshape(1, -1).astype(F32)
    return dict(
        w_in=pack_in(w_in).astype(MXU_DTYPE), b_in=row(pack_in(b_in)),
        mh_g=row(mh_norm_g), w_pool=w_pool.astype(MXU_DTYPE), pool_scale=row(pool_scale),
        w_out=w_out.astype(MXU_DTYPE), b_out=row(b_out), ln1_g=row(ln1_g), ln1_b=row(ln1_b),
        w_gate=w_up[:, :D_FF].astype(MXU_DTYPE), w_val=w_up[:, D_FF:].astype(MXU_DTYPE),
        b_gate=row(b_up[:D_FF]), b_val=row(b_up[D_FF:]),
        c_gate=w_conv[:, :D_FF].astype(F32), c_val=w_conv[:, D_FF:].astype(F32),
        bc_gate=row(b_conv[:D_FF]), bc_val=row(b_conv[D_FF:]),
        w_down=w_down.astype(MXU_DTYPE), b_down=row(b_down), ln2_g=row(ln2_g), ln2_b=row(ln2_b),
    )


def _tile(seq, want):
    return min(want, seq)


def _layer(x2d, p, batch, seq, alpha):
    q, k, v, o, u, gi, gf = _inproj(x2d, p["w_in"], p["b_in"], _tile(seq, 512))
    hf, hb = _mlstm(q, k, v, gi, gf, batch, seq)
    x1 = _outproj(hf, hb, o, u, x2d, p["mh_g"], p["w_pool"], p["pool_scale"], p["w_out"], p["b_out"],
                  p["ln1_g"], p["ln1_b"], seq, _tile(seq, 512), alpha)
    return _ffn(x1, p["w_gate"], p["w_val"], p["b_gate"], p["b_val"], p["c_gate"], p["c_val"],
                p["bc_gate"], p["bc_val"], p["w_down"], p["b_down"], p["ln2_g"], p["ln2_b"],
                seq, _tile(seq, 512), 512, alpha)


def kernel(x_prompt, x_sample, w_in, b_in, mh_norm_g, w_pool, pool_scale, w_out, b_out, ln1_g, ln1_b,
           w_up, b_up, w_conv, b_conv, w_down, b_down, ln2_g, ln2_b):
    weights = (w_in, b_in, mh_norm_g, w_pool, pool_scale, w_out, b_out, ln1_g, ln1_b,
               w_up, b_up, w_conv, b_conv, w_down, b_down, ln2_g, ln2_b)
    depth = w_in.shape[0]
    alpha = (2.0 * depth) ** 0.25
    layers = [_pack_layer(*(w[l] for w in weights)) for l in range(depth)]

    def trunk(x):
        batch, seq, _ = x.shape
        x2d = x.reshape(batch * seq, D_MODEL)
        for p in layers:
            x2d = _layer(x2d, p, batch, seq, alpha)
        return x2d.reshape(batch, seq, D_MODEL)

    return trunk(x_prompt), trunk(x_sample)
```

```python
import functools

import jax
import jax.numpy as jnp
from jax import lax
from jax.experimental import pallas as pl
from jax.experimental.pallas import tpu as pltpu

D_MODEL = 2048
D_MLSTM = 1024
N_HEADS = 4
HEAD_DIM = D_MLSTM // N_HEADS
D_POOL = D_MODEL - D_MLSTM
POOL_WINDOWS = (2, 4, 8, 16)
POOL_GROUP_DIM = D_POOL // len(POOL_WINDOWS)
D_FF = 5632
LN_EPS = 1e-5

LANES = 128
SUBLANES = 8
HALO = SUBLANES
CHUNK = 128
GATE_COLS = 2 * LANES
D_IN_PACKED = 4 * D_MLSTM + D_POOL + GATE_COLS
FF_TILE = 512
VMEM_LIMIT = 56 * 1024 * 1024

MXU_DTYPE = jnp.bfloat16
F32 = jnp.float32


def _mm(a, b):
    return jnp.dot(a, b, preferred_element_type=F32)


def _layer_norm_rows(y, g, b):
    mu = jnp.mean(y, axis=-1, keepdims=True)
    d = y - mu
    var = jnp.mean(d * d, axis=-1, keepdims=True)
    return d * lax.rsqrt(var + LN_EPS) * g + b


def _const_spec(shape):
    nd = len(shape)
    return pl.BlockSpec(shape, lambda *_: (0,) * nd, pipeline_mode=pl.Buffered(1))


def _inproj_kernel(x_ref, w_ref, b_ref, q_ref, k_ref, v_ref, o_ref, u_ref, gi_ref, gf_ref):
    xb = x_ref[...].astype(MXU_DTYPE)

    def proj(c0, c1):
        return _mm(xb, w_ref[:, c0:c1]) + b_ref[:, c0:c1]

    d = D_MLSTM
    q_ref[...] = proj(0, d).astype(q_ref.dtype)
    k_ref[...] = (proj(d, 2 * d) * (HEAD_DIM ** -0.5)).astype(k_ref.dtype)
    v_ref[...] = proj(2 * d, 3 * d).astype(v_ref.dtype)
    o_ref[...] = proj(3 * d, 4 * d)
    u_ref[...] = proj(4 * d, 4 * d + D_POOL)
    g = proj(4 * d + D_POOL, D_IN_PACKED)
    gi_ref[...] = g[:, :LANES]
    fp = g[:, LANES:]
    gf_ref[...] = jnp.minimum(fp, 0.0) - jnp.log1p(jnp.exp(-jnp.abs(fp)))


def _inproj(x2d, w_packed, b_packed, tm):
    n = x2d.shape[0]
    row = lambda w: pl.BlockSpec((tm, w), lambda i: (i, 0))
    out_shape = (
        jax.ShapeDtypeStruct((n, D_MLSTM), MXU_DTYPE),
        jax.ShapeDtypeStruct((n, D_MLSTM), MXU_DTYPE),
        jax.ShapeDtypeStruct((n, D_MLSTM), MXU_DTYPE),
        jax.ShapeDtypeStruct((n, D_MLSTM), F32),
        jax.ShapeDtypeStruct((n, D_POOL), F32),
        jax.ShapeDtypeStruct((n, LANES), F32),
        jax.ShapeDtypeStruct((n, LANES), F32),
    )
    return pl.pallas_call(
        _inproj_kernel,
        grid=(n // tm,),
        in_specs=[row(D_MODEL), _const_spec((D_MODEL, D_IN_PACKED)), _const_spec((1, D_IN_PACKED))],
        out_specs=(row(D_MLSTM), row(D_MLSTM), row(D_MLSTM), row(D_MLSTM), row(D_POOL),
                   row(LANES), row(LANES)),
        out_shape=out_shape,
        compiler_params=pltpu.CompilerParams(
            dimension_semantics=("parallel",), vmem_limit_bytes=VMEM_LIMIT),
        name="inproj",
    )(x2d, w_packed, b_packed)


def _split3(x):
    hi = x.astype(MXU_DTYPE)
    r1 = x - hi.astype(F32)
    mid = r1.astype(MXU_DTYPE)
    lo = (r1 - mid.astype(F32)).astype(MXU_DTYPE)
    return hi, mid, lo


def _cumsum_rows(tri, x):
    hi, mid, lo = _split3(x)
    return _mm(tri, hi) + _mm(tri, mid) + _mm(tri, lo)


def _cummax_rows(x, reverse):
    n = x.shape[0]
    ridx = lax.broadcasted_iota(jnp.int32, x.shape, 0)
    sh = 1
    while sh < n:
        if reverse:
            shifted = pltpu.roll(x, n - sh, axis=0)
            valid = ridx < n - sh
        else:
            shifted = pltpu.roll(x, sh, axis=0)
            valid = ridx >= sh
        x = jnp.maximum(x, jnp.where(valid, shifted, -jnp.inf))
        sh *= 2
    return x


def _mlstm_kernel(qf_ref, kf_ref, vf_ref, gif_ref, gff_ref,
                  qb_ref, kb_ref, vb_ref, gib_ref, gfb_ref,
                  hf_ref, hb_ref, c_ref, m_ref):
    L = CHUNK

    @pl.when(pl.program_id(1) == 0)
    def _():
        c_ref[...] = jnp.zeros_like(c_ref)
        m_ref[...] = jnp.zeros_like(m_ref)

    t_idx = lax.broadcasted_iota(jnp.int32, (L, L), 0)
    s_idx = lax.broadcasted_iota(jnp.int32, (L, L), 1)
    m_prev = m_ref[0:1, :]
    ones_ext = jnp.ones((L, LANES), MXU_DTYPE)
    m_new_rows = []

    for direction, (q_ref, k_ref, v_ref, gi_ref, gf_ref, h_ref) in enumerate((
            (qf_ref, kf_ref, vf_ref, gif_ref, gff_ref, hf_ref),
            (qb_ref, kb_ref, vb_ref, gib_ref, gfb_ref, hb_ref))):
        reverse = direction == 1
        mask = (s_idx >= t_idx) if reverse else (s_idx <= t_idx)
        last = 0 if reverse else L - 1
        b = _cumsum_rows(mask.astype(MXU_DTYPE), gf_ref[...])
        a = gi_ref[...] - b
        big_m = jnp.maximum(_cummax_rows(a, reverse), m_prev)
        w_inter = jnp.exp(m_prev - big_m)
        exp_neg_m = jnp.exp(-(b + big_m))
        m_last = big_m[last:last + 1, :]
        decay = jnp.exp(m_prev - m_last)
        w_state = jnp.exp(a - m_last)
        m_new_rows.append(b[last:last + 1, :] + m_last)
        a_t = a.T

        for h in range(N_HEADS):
            lane = direction * N_HEADS + h
            cols = slice(h * HEAD_DIM, (h + 1) * HEAD_DIM)
            q = q_ref[:, cols]
            k = k_ref[:, cols]
            v_ext = jnp.concatenate([v_ref[:, cols], ones_ext], axis=1)
            s_qk = lax.dot_general(q, k, (((1,), (1,)), ((), ())), preferred_element_type=F32)
            logw = jnp.where(mask, a_t[lane:lane + 1, :] - big_m[:, lane:lane + 1], -jnp.inf)
            s = (s_qk * jnp.exp(logw)).astype(MXU_DTYPE)
            c_old = c_ref[lane]
            r = _mm(s, v_ext) + w_inter[:, lane:lane + 1] * _mm(q, c_old.astype(MXU_DTYPE))
            den = jnp.maximum(jnp.abs(r[:, HEAD_DIM:]), exp_neg_m[:, lane:lane + 1])
            inv = 1.0 / den
            h_ref[:, cols] = r[:, :HEAD_DIM] * jnp.concatenate([inv] * (HEAD_DIM // LANES), axis=1)
            kw = (k.astype(F32) * w_state[:, lane:lane + 1]).astype(MXU_DTYPE)
            upd = lax.dot_general(kw, v_ext, (((0,), (0,)), ((), ())), preferred_element_type=F32)
            c_ref[lane] = decay[:, lane:lane + 1] * c_old + upd

    lane_idx = lax.broadcasted_iota(jnp.int32, (1, LANES), 1)
    m_ref[0:1, :] = jnp.where(lane_idx < N_HEADS, m_new_rows[0], m_new_rows[1])


def _mlstm(q, k, v, gi, gf, batch, seq):
    n = q.shape[0]
    nc = seq // CHUNK
    fwd = lambda w: pl.BlockSpec((CHUNK, w), lambda b, c: (b * nc + c, 0))
    bwd = lambda w: pl.BlockSpec((CHUNK, w), lambda b, c: (b * nc + nc - 1 - c, 0))
    return pl.pallas_call(
        _mlstm_kernel,
        grid=(batch, nc),
        in_specs=[fwd(D_MLSTM), fwd(D_MLSTM), fwd(D_MLSTM), fwd(LANES), fwd(LANES),
                  bwd(D_MLSTM), bwd(D_MLSTM), bwd(D_MLSTM), bwd(LANES), bwd(LANES)],
        out_specs=(fwd(D_MLSTM), bwd(D_MLSTM)),
        out_shape=(jax.ShapeDtypeStruct((n, D_MLSTM), F32), jax.ShapeDtypeStruct((n, D_MLSTM), F32)),
        scratch_shapes=[pltpu.VMEM((2 * N_HEADS, HEAD_DIM, HEAD_DIM + LANES), F32),
                        pltpu.VMEM((SUBLANES, LANES), F32)],
        compiler_params=pltpu.CompilerParams(
            dimension_semantics=("parallel", "arbitrary"), vmem_limit_bytes=VMEM_LIMIT),
        name="mlstm_scan",
    )(q, k, v, gi, gf, q, k, v, gi, gf)


def _outproj_kernel(hf_ref, hb_ref, o_ref, u_ref, up_ref, un_ref, x_ref,
                    mhg_ref, wp_ref, ps_ref, wo_ref, bo_ref, g_ref, b_ref,
                    y_ref, uext_ref, *, tiles_per_seq, seq, alpha):
    tm = x_ref.shape[0]
    t_in_seq = pl.program_id(0) % tiles_per_seq
    not_first = (t_in_seq > 0).astype(F32)
    not_last = (t_in_seq < tiles_per_seq - 1).astype(F32)

    hs = hf_ref[...] + hb_ref[...]
    parts = []
    for h in range(N_HEADS):
        xh = hs[:, h * HEAD_DIM:(h + 1) * HEAD_DIM]
        mu = jnp.mean(xh, axis=-1, keepdims=True)
        d = xh - mu
        var = jnp.mean(d * d, axis=-1, keepdims=True)
        parts.append(d * lax.rsqrt(var + LN_EPS))
    hm = jnp.concatenate(parts, axis=1) * mhg_ref[...] * jax.nn.sigmoid(o_ref[...])

    u = u_ref[...]
    uext_ref[0:HALO, :] = up_ref[...] * not_first
    uext_ref[HALO:HALO + tm, :] = u
    uext_ref[HALO + tm:, :] = un_ref[...] * not_last
    pos = t_in_seq * tm + lax.broadcasted_iota(jnp.int32, (tm, 1), 0)
    hp_parts = []
    for gi, w in enumerate(POOL_WINDOWS):
        cols = slice(gi * POOL_GROUP_DIM, (gi + 1) * POOL_GROUP_DIM)
        acc = uext_ref[HALO - w // 2:HALO - w // 2 + tm, cols]
        for j in range(-(w // 2) + 1, w - w // 2):
            acc = acc + uext_ref[HALO + j:HALO + j + tm, cols]
        cnt = (jnp.minimum(pos + (w - w // 2), seq) - jnp.maximum(pos - w // 2, 0)).astype(F32)
        pooled = acc / cnt - u[:, cols]
        hp_parts.append(_mm(pooled.astype(MXU_DTYPE), wp_ref[gi]))
    hp = jnp.concatenate(hp_parts, axis=1) * ps_ref[...]

    mixed = jnp.concatenate([hm, hp], axis=1).astype(MXU_DTYPE)
    half = D_MODEL // 2
    mix = jnp.concatenate([_mm(mixed, wo_ref[:, :half]), _mm(mixed, wo_ref[:, half:])],
                          axis=1) + bo_ref[...]
    y_ref[...] = _layer_norm_rows(alpha * x_ref[...] + mix, g_ref[...], b_ref[...])


def _outproj(hf, hb, o, u, x2d, mh_g, w_pool, pool_scale, w_out, b_out, ln_g, ln_b, seq, tm, alpha):
    n = x2d.shape[0]
    tiles_per_seq = seq // tm
    hb_per_tile = tm // HALO
    n_halo_blocks = n // HALO
    row = lambda w: pl.BlockSpec((tm, w), lambda i: (i, 0))
    prev = pl.BlockSpec((HALO, D_POOL), lambda i: (jnp.maximum(i * hb_per_tile - 1, 0), 0))
    nxt = pl.BlockSpec((HALO, D_POOL),
                       lambda i: (jnp.minimum((i + 1) * hb_per_tile, n_halo_blocks - 1), 0))
    kern = functools.partial(_outproj_kernel, tiles_per_seq=tiles_per_seq, seq=seq, alpha=alpha)
    return pl.pallas_call(
        kern,
        grid=(n // tm,),
        in_specs=[row(D_MLSTM), row(D_MLSTM), row(D_MLSTM), row(D_POOL), prev, nxt, row(D_MODEL),
                  _const_spec((1, D_MLSTM)),
                  _const_spec((len(POOL_WINDOWS), POOL_GROUP_DIM, POOL_GROUP_DIM)),
                  _const_spec((1, D_POOL)), _const_spec((D_MODEL, D_MODEL)),
                  _const_spec((1, D_MODEL)), _const_spec((1, D_MODEL)), _const_spec((1, D_MODEL))],
        out_specs=row(D_MODEL),
        out_shape=jax.ShapeDtypeStruct((n, D_MODEL), F32),
        scratch_shapes=[pltpu.VMEM((tm + 2 * HALO, D_POOL), F32)],
        compiler_params=pltpu.CompilerParams(
            dimension_semantics=("parallel",), vmem_limit_bytes=VMEM_LIMIT),
        name="outproj_ln",
    )(hf, hb, o, u, u, u, x2d, mh_g, w_pool, pool_scale, w_out, b_out, ln_g, ln_b)


def _ffn_kernel(x_ref, xp_ref, xn_ref, wu_ref, vec_ref, wd_ref, bd_ref, g_ref, b_ref,
                y_ref, xs_ref, a_ref, *, tiles_per_seq, alpha):
    tm = x_ref.shape[0]
    tf = wd_ref.shape[0]
    j = pl.program_id(1)
    t_in_seq = pl.program_id(0) % tiles_per_seq
    not_first = (t_in_seq > 0).astype(F32)
    not_last = (t_in_seq < tiles_per_seq - 1).astype(F32)

    @pl.when(j == 0)
    def _():
        x = x_ref[...]
        xs_ref[...] = jnp.concatenate([xp_ref[...], x, xn_ref[...]], axis=0).astype(xs_ref.dtype)
        y_ref[...] = alpha * x + bd_ref[...]

    xs = xs_ref[...]
    conv = []
    for cols in (slice(0, tf), slice(tf, 2 * tf)):
        a = _mm(xs, wu_ref[:, cols]) + vec_ref[0:1, cols]
        a_ref[0:HALO, cols] = a[0:HALO] * not_first
        a_ref[HALO:HALO + tm, cols] = a[HALO:HALO + tm]
        a_ref[HALO + tm:, cols] = a[HALO + tm:] * not_last
        conv.append(vec_ref[1:2, cols] * a_ref[HALO - 1:HALO - 1 + tm, cols]
                    + vec_ref[2:3, cols] * a_ref[HALO:HALO + tm, cols]
                    + vec_ref[3:4, cols] * a_ref[HALO + 1:HALO + 1 + tm, cols] + vec_ref[4:5, cols])
    gate, val = conv
    f = (gate * jax.nn.sigmoid(gate) * val).astype(MXU_DTYPE)
    for cols in (slice(0, D_MODEL // 2), slice(D_MODEL // 2, D_MODEL)):
        y_ref[:, cols] += _mm(f, wd_ref[:, cols])

    @pl.when(j == pl.num_programs(1) - 1)
    def _():
        y_ref[...] = _layer_norm_rows(y_ref[...], g_ref[...], b_ref[...])


def _ffn(x1, w_up, ff_vecs, w_down, b_down, ln_g, ln_b, seq, tm, tf, alpha):
    n = x1.shape[0]
    tiles_per_seq = seq // tm
    hb_per_tile = tm // HALO
    n_halo_blocks = n // HALO
    nj = D_FF // tf
    main = pl.BlockSpec((tm, D_MODEL), lambda i, j: (i, 0))
    prev = pl.BlockSpec((HALO, D_MODEL), lambda i, j: (jnp.maximum(i * hb_per_tile - 1, 0), 0))
    nxt = pl.BlockSpec((HALO, D_MODEL),
                       lambda i, j: (jnp.minimum((i + 1) * hb_per_tile, n_halo_blocks - 1), 0))
    up = pl.BlockSpec((D_MODEL, 2 * tf), lambda i, j: (0, j))
    vecs = pl.BlockSpec((SUBLANES, 2 * tf), lambda i, j: (0, j))
    down = pl.BlockSpec((tf, D_MODEL), lambda i, j: (j, 0))
    vec = pl.BlockSpec((1, D_MODEL), lambda i, j: (0, 0))
    kern = functools.partial(_ffn_kernel, tiles_per_seq=tiles_per_seq, alpha=alpha)
    return pl.pallas_call(
        kern,
        grid=(n // tm, nj),
        in_specs=[main, prev, nxt, up, vecs, down, vec, vec, vec],
        out_specs=main,
        out_shape=jax.ShapeDtypeStruct((n, D_MODEL), F32),
        scratch_shapes=[pltpu.VMEM((tm + 2 * HALO, D_MODEL), MXU_DTYPE),
                        pltpu.VMEM((tm + 2 * HALO, 2 * tf), F32)],
        compiler_params=pltpu.CompilerParams(
            dimension_semantics=("parallel", "arbitrary"), vmem_limit_bytes=VMEM_LIMIT),
        name="ffn_ln",
    )(x1, x1, x1, w_up, ff_vecs, w_down, b_down, ln_g, ln_b)


def _pack_layer(w_in, b_in, mh_norm_g, w_pool, pool_scale, w_out, b_out, ln1_g, ln1_b,
                w_up, b_up, w_conv, b_conv, w_down, b_down, ln2_g, ln2_b):
    g0 = 4 * D_MLSTM
    h = N_HEADS
    ng = 4 * h

    def gate_cols(a):
        gts = a[..., g0:g0 + ng]
        pad = jnp.zeros(a.shape[:-1] + (LANES - 2 * h,), a.dtype)
        return jnp.concatenate([gts[..., 0:h], gts[..., 2 * h:3 * h], pad,
                                gts[..., h:2 * h], gts[..., 3 * h:4 * h], pad], axis=-1)

    pack_in = lambda a: jnp.concatenate([a[..., :g0], a[..., g0 + ng:], gate_cols(a)], axis=-1)
    row = lambda a: a.reshape(1, -1).astype(F32)

    def ff_tiles(a):
        r = a.shape[0]
        parts = a.reshape(r, 2, D_FF // FF_TILE, FF_TILE)
        return jnp.swapaxes(parts, 1, 2).reshape(r, 2 * D_FF)

    return dict(
        w_in=pack_in(w_in).astype(MXU_DTYPE), b_in=row(pack_in(b_in)),
        mh_g=row(mh_norm_g), w_pool=w_pool.astype(MXU_DTYPE), pool_scale=row(pool_scale),
        w_out=w_out.astype(MXU_DTYPE), b_out=row(b_out), ln1_g=row(ln1_g), ln1_b=row(ln1_b),
        w_up=ff_tiles(w_up).astype(MXU_DTYPE),
        ff_vecs=ff_tiles(jnp.concatenate(
            [b_up[None], w_conv, b_conv[None], jnp.zeros((SUBLANES - 5, 2 * D_FF), F32)],
            axis=0).astype(F32)),
        w_down=w_down.astype(MXU_DTYPE), b_down=row(b_down), ln2_g=row(ln2_g), ln2_b=row(ln2_b),
    )


def _tile(seq, want):
    return min(want, seq)


def _layer(x2d, p, batch, seq, alpha):
    q, k, v, o, u, gi, gf = _inproj(x2d, p["w_in"], p["b_in"], _tile(seq, 512))
    hf, hb = _mlstm(q, k, v, gi, gf, batch, seq)
    x1 = _outproj(hf, hb, o, u, x2d, p["mh_g"], p["w_pool"], p["pool_scale"], p["w_out"], p["b_out"],
                  p["ln1_g"], p["ln1_b"], seq, _tile(seq, 512), alpha)
    return _ffn(x1, p["w_up"], p["ff_vecs"], p["w_down"], p["b_down"], p["ln2_g"], p["ln2_b"],
                seq, _tile(seq, 512), FF_TILE, alpha)


def kernel(x_prompt, x_sample, w_in, b_in, mh_norm_g, w_pool, pool_scale, w_out, b_out, ln1_g, ln1_b,
           w_up, b_up, w_conv, b_conv, w_down, b_down, ln2_g, ln2_b):
    weights = (w_in, b_in, mh_norm_g, w_pool, pool_scale, w_out, b_out, ln1_g, ln1_b,
               w_up, b_up, w_conv, b_conv, w_down, b_down, ln2_g, ln2_b)
    depth = w_in.shape[0]
    alpha = (2.0 * depth) ** 0.25
    layers = [_pack_layer(*(w[l] for w in weights)) for l in range(depth)]

    def trunk(x):
        batch, seq, _ = x.shape
        x2d = x.reshape(batch * seq, D_MODEL)
        for p in layers:
            x2d = _layer(x2d, p, batch, seq, alpha)
        return x2d.reshape(batch, seq, D_MODEL)

    return trunk(x_prompt), trunk(x_sample)
```

```python
import functools

import jax
import jax.numpy as jnp
from jax import lax
from jax.experimental import pallas as pl
from jax.experimental.pallas import tpu as pltpu

D_MODEL = 2048
D_MLSTM = 1024
N_HEADS = 4
HEAD_DIM = D_MLSTM // N_HEADS
D_POOL = D_MODEL - D_MLSTM
POOL_WINDOWS = (2, 4, 8, 16)
POOL_GROUP_DIM = D_POOL // len(POOL_WINDOWS)
D_FF = 5632
LN_EPS = 1e-5

LANES = 128
SUBLANES = 8
HALO = SUBLANES
CHUNK = 128
GATE_COLS = 2 * LANES
D_IN_PACKED = 4 * D_MLSTM + D_POOL + GATE_COLS
FF_TILE = 512
VMEM_LIMIT = 56 * 1024 * 1024

MXU_DTYPE = jnp.bfloat16
F32 = jnp.float32


def _mm(a, b):
    return jnp.dot(a, b, preferred_element_type=F32)


def _layer_norm_rows(y, g, b):
    mu = jnp.mean(y, axis=-1, keepdims=True)
    d = y - mu
    var = jnp.mean(d * d, axis=-1, keepdims=True)
    return d * lax.rsqrt(var + LN_EPS) * g + b


def _const_spec(shape):
    nd = len(shape)
    return pl.BlockSpec(shape, lambda *_: (0,) * nd, pipeline_mode=pl.Buffered(1))


def _inproj_kernel(x_ref, w_ref, b_ref, q_ref, k_ref, v_ref, o_ref, u_ref, gi_ref, gf_ref):
    xb = x_ref[...].astype(MXU_DTYPE)

    def proj(c0, c1):
        return _mm(xb, w_ref[:, c0:c1]) + b_ref[:, c0:c1]

    d = D_MLSTM
    q_ref[...] = proj(0, d).astype(q_ref.dtype)
    k_ref[...] = (proj(d, 2 * d) * (HEAD_DIM ** -0.5)).astype(k_ref.dtype)
    v_ref[...] = proj(2 * d, 3 * d).astype(v_ref.dtype)
    o_ref[...] = proj(3 * d, 4 * d)
    u_ref[...] = proj(4 * d, 4 * d + D_POOL)
    g = proj(4 * d + D_POOL, D_IN_PACKED)
    gi_ref[...] = g[:, :LANES]
    fp = g[:, LANES:]
    gf_ref[...] = jnp.minimum(fp, 0.0) - jnp.log1p(jnp.exp(-jnp.abs(fp)))


def _inproj(x2d, w_packed, b_packed, tm):
    n = x2d.shape[0]
    row = lambda w: pl.BlockSpec((tm, w), lambda i: (i, 0))
    out_shape = (
        jax.ShapeDtypeStruct((n, D_MLSTM), MXU_DTYPE),
        jax.ShapeDtypeStruct((n, D_MLSTM), MXU_DTYPE),
        jax.ShapeDtypeStruct((n, D_MLSTM), MXU_DTYPE),
        jax.ShapeDtypeStruct((n, D_MLSTM), F32),
        jax.ShapeDtypeStruct((n, D_POOL), F32),
        jax.ShapeDtypeStruct((n, LANES), F32),
        jax.ShapeDtypeStruct((n, LANES), F32),
    )
    return pl.pallas_call(
        _inproj_kernel,
        grid=(n // tm,),
        in_specs=[row(D_MODEL), _const_spec((D_MODEL, D_IN_PACKED)), _const_spec((1, D_IN_PACKED))],
        out_specs=(row(D_MLSTM), row(D_MLSTM), row(D_MLSTM), row(D_MLSTM), row(D_POOL),
                   row(LANES), row(LANES)),
        out_shape=out_shape,
        compiler_params=pltpu.CompilerParams(
            dimension_semantics=("parallel",), vmem_limit_bytes=VMEM_LIMIT),
        name="inproj",
    )(x2d, w_packed, b_packed)


def _split3(x):
    hi = x.astype(MXU_DTYPE)
    r1 = x - hi.astype(F32)
    mid = r1.astype(MXU_DTYPE)
    lo = (r1 - mid.astype(F32)).astype(MXU_DTYPE)
    return hi, mid, lo


def _cumsum_rows(tri, x):
    hi, mid, lo = _split3(x)
    return _mm(tri, hi) + _mm(tri, mid) + _mm(tri, lo)


def _cummax_rows(x, reverse):
    n = x.shape[0]
    ridx = lax.broadcasted_iota(jnp.int32, x.shape, 0)
    sh = 1
    while sh < n:
        if reverse:
            shifted = pltpu.roll(x, n - sh, axis=0)
            valid = ridx < n - sh
        else:
            shifted = pltpu.roll(x, sh, axis=0)
            valid = ridx >= sh
        x = jnp.maximum(x, jnp.where(valid, shifted, -jnp.inf))
        sh *= 2
    return x


def _mlstm_kernel(qf_ref, kf_ref, vf_ref, gif_ref, gff_ref,
                  qb_ref, kb_ref, vb_ref, gib_ref, gfb_ref,
                  hf_ref, hb_ref, c_ref, m_ref):
    L = CHUNK

    @pl.when(pl.program_id(1) == 0)
    def _():
        c_ref[...] = jnp.zeros_like(c_ref)
        m_ref[...] = jnp.zeros_like(m_ref)

    t_idx = lax.broadcasted_iota(jnp.int32, (L, L), 0)
    s_idx = lax.broadcasted_iota(jnp.int32, (L, L), 1)
    m_prev = m_ref[0:1, :]
    ones_ext = jnp.ones((L, LANES), MXU_DTYPE)
    m_new_rows = []

    for direction, (q_ref, k_ref, v_ref, gi_ref, gf_ref, h_ref) in enumerate((
            (qf_ref, kf_ref, vf_ref, gif_ref, gff_ref, hf_ref),
            (qb_ref, kb_ref, vb_ref, gib_ref, gfb_ref, hb_ref))):
        reverse = direction == 1
        mask = (s_idx >= t_idx) if reverse else (s_idx <= t_idx)
        last = 0 if reverse else L - 1
        b = _cumsum_rows(mask.astype(MXU_DTYPE), gf_ref[...])
        a = gi_ref[...] - b
        big_m = jnp.maximum(_cummax_rows(a, reverse), m_prev)
        w_inter = jnp.exp(m_prev - big_m)
        exp_neg_m = jnp.exp(-(b + big_m))
        m_last = big_m[last:last + 1, :]
        decay = jnp.exp(m_prev - m_last)
        w_state = jnp.exp(a - m_last)
        m_new_rows.append(b[last:last + 1, :] + m_last)
        a_t = a.T

        for h in range(N_HEADS):
            lane = direction * N_HEADS + h
            cols = slice(h * HEAD_DIM, (h + 1) * HEAD_DIM)
            q = q_ref[:, cols]
            k = k_ref[:, cols]
            v_ext = jnp.concatenate([v_ref[:, cols], ones_ext], axis=1)
            s_qk = lax.dot_general(q, k, (((1,), (1,)), ((), ())), preferred_element_type=F32)
            logw = jnp.where(mask, a_t[lane:lane + 1, :] - big_m[:, lane:lane + 1], -jnp.inf)
            s = (s_qk * jnp.exp(logw)).astype(MXU_DTYPE)
            c_old = c_ref[lane]
            r = _mm(s, v_ext) + w_inter[:, lane:lane + 1] * _mm(q, c_old.astype(MXU_DTYPE))
            den = jnp.maximum(jnp.abs(r[:, HEAD_DIM:]), exp_neg_m[:, lane:lane + 1])
            inv = 1.0 / den
            h_ref[:, cols] = r[:, :HEAD_DIM] * jnp.concatenate([inv] * (HEAD_DIM // LANES), axis=1)
            kw = (k.astype(F32) * w_state[:, lane:lane + 1]).astype(MXU_DTYPE)
            upd = lax.dot_general(kw, v_ext, (((0,), (0,)), ((), ())), preferred_element_type=F32)
            c_ref[lane] = decay[:, lane:lane + 1] * c_old + upd

    lane_idx = lax.broadcasted_iota(jnp.int32, (1, LANES), 1)
    m_ref[0:1, :] = jnp.where(lane_idx < N_HEADS, m_new_rows[0], m_new_rows[1])


def _mlstm(q, k, v, gi, gf, batch, seq):
    n = q.shape[0]
    nc = seq // CHUNK
    fwd = lambda w: pl.BlockSpec((CHUNK, w), lambda b, c: (b * nc + c, 0))
    bwd = lambda w: pl.BlockSpec((CHUNK, w), lambda b, c: (b * nc + nc - 1 - c, 0))
    return pl.pallas_call(
        _mlstm_kernel,
        grid=(batch, nc),
        in_specs=[fwd(D_MLSTM), fwd(D_MLSTM), fwd(D_MLSTM), fwd(LANES), fwd(LANES),
                  bwd(D_MLSTM), bwd(D_MLSTM), bwd(D_MLSTM), bwd(LANES), bwd(LANES)],
        out_specs=(fwd(D_MLSTM), bwd(D_MLSTM)),
        out_shape=(jax.ShapeDtypeStruct((n, D_MLSTM), F32), jax.ShapeDtypeStruct((n, D_MLSTM), F32)),
        scratch_shapes=[pltpu.VMEM((2 * N_HEADS, HEAD_DIM, HEAD_DIM + LANES), F32),
                        pltpu.VMEM((SUBLANES, LANES), F32)],
        compiler_params=pltpu.CompilerParams(
            dimension_semantics=("parallel", "arbitrary"), vmem_limit_bytes=VMEM_LIMIT),
        name="mlstm_scan",
    )(q, k, v, gi, gf, q, k, v, gi, gf)


def _outproj_kernel(hf_ref, hb_ref, o_ref, u_ref, up_ref, un_ref, x_ref,
                    mhg_ref, wp_ref, ps_ref, wo_ref, bo_ref, g_ref, b_ref,
                    y_ref, uext_ref, *, tiles_per_seq, seq, alpha):
    tm = x_ref.shape[0]
    t_in_seq = pl.program_id(0) % tiles_per_seq
    not_first = (t_in_seq > 0).astype(F32)
    not_last = (t_in_seq < tiles_per_seq - 1).astype(F32)

    hs = hf_ref[...] + hb_ref[...]
    parts = []
    for h in range(N_HEADS):
        xh = hs[:, h * HEAD_DIM:(h + 1) * HEAD_DIM]
        mu = jnp.mean(xh, axis=-1, keepdims=True)
        d = xh - mu
        var = jnp.mean(d * d, axis=-1, keepdims=True)
        parts.append(d * lax.rsqrt(var + LN_EPS))
    hm = jnp.concatenate(parts, axis=1) * mhg_ref[...] * jax.nn.sigmoid(o_ref[...])

    u = u_ref[...]
    uext_ref[0:HALO, :] = up_ref[...] * not_first
    uext_ref[HALO:HALO + tm, :] = u
    uext_ref[HALO + tm:, :] = un_ref[...] * not_last
    pos = t_in_seq * tm + lax.broadcasted_iota(jnp.int32, (tm, 1), 0)
    hp_parts = []
    for gi, w in enumerate(POOL_WINDOWS):
        cols = slice(gi * POOL_GROUP_DIM, (gi + 1) * POOL_GROUP_DIM)
        acc = uext_ref[HALO - w // 2:HALO - w // 2 + tm, cols]
        for j in range(-(w // 2) + 1, w - w // 2):
            acc = acc + uext_ref[HALO + j:HALO + j + tm, cols]
        cnt = (jnp.minimum(pos + (w - w // 2), seq) - jnp.maximum(pos - w // 2, 0)).astype(F32)
        pooled = acc / cnt - u[:, cols]
        hp_parts.append(_mm(pooled.astype(MXU_DTYPE), wp_ref[gi]))
    hp = jnp.concatenate(hp_parts, axis=1) * ps_ref[...]

    mixed = jnp.concatenate([hm, hp], axis=1).astype(MXU_DTYPE)
    half = D_MODEL // 2
    mix = jnp.concatenate([_mm(mixed, wo_ref[:, :half]), _mm(mixed, wo_ref[:, half:])],
                          axis=1) + bo_ref[...]
    y_ref[...] = _layer_norm_rows(alpha * x_ref[...] + mix, g_ref[...], b_ref[...])


def _outproj(hf, hb, o, u, x2d, mh_g, w_pool, pool_scale, w_out, b_out, ln_g, ln_b, seq, tm, alpha):
    n = x2d.shape[0]
    tiles_per_seq = seq // tm
    hb_per_tile = tm // HALO
    n_halo_blocks = n // HALO
    row = lambda w: pl.BlockSpec((tm, w), lambda i: (i, 0))
    prev = pl.BlockSpec((HALO, D_POOL), lambda i: (jnp.maximum(i * hb_per_tile - 1, 0), 0))
    nxt = pl.BlockSpec((HALO, D_POOL),
                       lambda i: (jnp.minimum((i + 1) * hb_per_tile, n_halo_blocks - 1), 0))
    kern = functools.partial(_outproj_kernel, tiles_per_seq=tiles_per_seq, seq=seq, alpha=alpha)
    return pl.pallas_call(
        kern,
        grid=(n // tm,),
        in_specs=[row(D_MLSTM), row(D_MLSTM), row(D_MLSTM), row(D_POOL), prev, nxt, row(D_MODEL),
                  _const_spec((1, D_MLSTM)),
                  _const_spec((len(POOL_WINDOWS), POOL_GROUP_DIM, POOL_GROUP_DIM)),
                  _const_spec((1, D_POOL)), _const_spec((D_MODEL, D_MODEL)),
                  _const_spec((1, D_MODEL)), _const_spec((1, D_MODEL)), _const_spec((1, D_MODEL))],
        out_specs=row(D_MODEL),
        out_shape=jax.ShapeDtypeStruct((n, D_MODEL), F32),
        scratch_shapes=[pltpu.VMEM((tm + 2 * HALO, D_POOL), F32)],
        compiler_params=pltpu.CompilerParams(
            dimension_semantics=("parallel",), vmem_limit_bytes=VMEM_LIMIT),
        name="outproj_ln",
    )(hf, hb, o, u, u, u, x2d, mh_g, w_pool, pool_scale, w_out, b_out, ln_g, ln_b)


def _ffn_kernel(x_ref, xp_ref, xn_ref, wg_ref, wv_ref, vg_ref, vv_ref, wd_ref, bd_ref, g_ref, b_ref,
                y_ref, xs_ref, a_ref, *, tiles_per_seq, alpha):
    tm = x_ref.shape[0]
    tf = wd_ref.shape[0]
    j = pl.program_id(1)
    t_in_seq = pl.program_id(0) % tiles_per_seq
    not_first = (t_in_seq > 0).astype(F32)
    not_last = (t_in_seq < tiles_per_seq - 1).astype(F32)

    @pl.when(j == 0)
    def _():
        x = x_ref[...]
        xs_ref[...] = jnp.concatenate([xp_ref[...], x, xn_ref[...]], axis=0).astype(xs_ref.dtype)
        y_ref[...] = alpha * x + bd_ref[...]

    xs = xs_ref[...]
    conv = []
    for part, (w_ref, vec_ref) in enumerate(((wg_ref, vg_ref), (wv_ref, vv_ref))):
        cols = slice(part * tf, (part + 1) * tf)
        a = _mm(xs, w_ref[...]) + vec_ref[0:1, :]
        a_ref[0:HALO, cols] = a[0:HALO] * not_first
        a_ref[HALO:HALO + tm, cols] = a[HALO:HALO + tm]
        a_ref[HALO + tm:, cols] = a[HALO + tm:] * not_last
        conv.append(vec_ref[1:2, :] * a_ref[HALO - 1:HALO - 1 + tm, cols]
                    + vec_ref[2:3, :] * a_ref[HALO:HALO + tm, cols]
                    + vec_ref[3:4, :] * a_ref[HALO + 1:HALO + 1 + tm, cols] + vec_ref[4:5, :])
    gate, val = conv
    f = (gate * jax.nn.sigmoid(gate) * val).astype(MXU_DTYPE)
    for cols in (slice(0, D_MODEL // 2), slice(D_MODEL // 2, D_MODEL)):
        y_ref[:, cols] += _mm(f, wd_ref[:, cols])

    @pl.when(j == pl.num_programs(1) - 1)
    def _():
        y_ref[...] = _layer_norm_rows(y_ref[...], g_ref[...], b_ref[...])


def _ffn(x1, w_up, ff_vecs, w_down, b_down, ln_g, ln_b, seq, tm, tf, alpha):
    n = x1.shape[0]
    tiles_per_seq = seq // tm
    hb_per_tile = tm // HALO
    n_halo_blocks = n // HALO
    nj = D_FF // tf
    main = pl.BlockSpec((tm, D_MODEL), lambda i, j: (i, 0))
    prev = pl.BlockSpec((HALO, D_MODEL), lambda i, j: (jnp.maximum(i * hb_per_tile - 1, 0), 0))
    nxt = pl.BlockSpec((HALO, D_MODEL),
                       lambda i, j: (jnp.minimum((i + 1) * hb_per_tile, n_halo_blocks - 1), 0))
    up_gate = pl.BlockSpec((D_MODEL, tf), lambda i, j: (0, j))
    up_val = pl.BlockSpec((D_MODEL, tf), lambda i, j: (0, j + nj))
    vec_gate = pl.BlockSpec((SUBLANES, tf), lambda i, j: (0, j))
    vec_val = pl.BlockSpec((SUBLANES, tf), lambda i, j: (0, j + nj))
    down = pl.BlockSpec((tf, D_MODEL), lambda i, j: (j, 0))
    vec = pl.BlockSpec((1, D_MODEL), lambda i, j: (0, 0))
    kern = functools.partial(_ffn_kernel, tiles_per_seq=tiles_per_seq, alpha=alpha)
    return pl.pallas_call(
        kern,
        grid=(n // tm, nj),
        in_specs=[main, prev, nxt, up_gate, up_val, vec_gate, vec_val, down, vec, vec, vec],
        out_specs=main,
        out_shape=jax.ShapeDtypeStruct((n, D_MODEL), F32),
        scratch_shapes=[pltpu.VMEM((tm + 2 * HALO, D_MODEL), MXU_DTYPE),
                        pltpu.VMEM((tm + 2 * HALO, 2 * tf), F32)],
        compiler_params=pltpu.CompilerParams(
            dimension_semantics=("parallel", "arbitrary"), vmem_limit_bytes=VMEM_LIMIT),
        name="ffn_ln",
    )(x1, x1, x1, w_up, w_up, ff_vecs, ff_vecs, w_down, b_down, ln_g, ln_b)


def _pack_layer(w_in, b_in, mh_norm_g, w_pool, pool_scale, w_out, b_out, ln1_g, ln1_b,
                w_up, b_up, w_conv, b_conv, w_down, b_down, ln2_g, ln2_b):
    g0 = 4 * D_MLSTM
    h = N_HEADS
    ng = 4 * h

    def gate_cols(a):
        gts = a[..., g0:g0 + ng]
        pad = jnp.zeros(a.shape[:-1] + (LANES - 2 * h,), a.dtype)
        return jnp.concatenate([gts[..., 0:h], gts[..., 2 * h:3 * h], pad,
                                gts[..., h:2 * h], gts[..., 3 * h:4 * h], pad], axis=-1)

    pack_in = lambda a: jnp.concatenate([a[..., :g0], a[..., g0 + ng:], gate_cols(a)], axis=-1)
    row = lambda a: a.reshape(1, -1).astype(F32)

    return dict(
        w_in=pack_in(w_in).astype(MXU_DTYPE), b_in=row(pack_in(b_in)),
        mh_g=row(mh_norm_g), w_pool=w_pool.astype(MXU_DTYPE), pool_scale=row(pool_scale),
        w_out=w_out.astype(MXU_DTYPE), b_out=row(b_out), ln1_g=row(ln1_g), ln1_b=row(ln1_b),
        w_up=w_up.astype(MXU_DTYPE),
        ff_vecs=jnp.concatenate(
            [b_up[None], w_conv, b_conv[None], jnp.zeros((SUBLANES - 5, 2 * D_FF), F32)],
            axis=0).astype(F32),
        w_down=w_down.astype(MXU_DTYPE), b_down=row(b_down), ln2_g=row(ln2_g), ln2_b=row(ln2_b),
    )


def _tile(seq, want):
    return min(want, seq)


def _layer(x2d, p, batch, seq, alpha):
    q, k, v, o, u, gi, gf = _inproj(x2d, p["w_in"], p["b_in"], _tile(seq, 512))
    hf, hb = _mlstm(q, k, v, gi, gf, batch, seq)
    x1 = _outproj(hf, hb, o, u, x2d, p["mh_g"], p["w_pool"], p["pool_scale"], p["w_out"], p["b_out"],
                  p["ln1_g"], p["ln1_b"], seq, _tile(seq, 512), alpha)
    return _ffn(x1, p["w_up"], p["ff_vecs"], p["w_down"], p["b_down"], p["ln2_g"], p["ln2_b"],
                seq, _tile(seq, 512), FF_TILE, alpha)


def kernel(x_prompt, x_sample, w_in, b_in, mh_norm_g, w_pool, pool_scale, w_out, b_out, ln1_g, ln1_b,
           w_up, b_up, w_conv, b_conv, w_down, b_down, ln2_g, ln2_b):
    weights = (w_in, b_in, mh_norm_g, w_pool, pool_scale, w_out, b_out, ln1_g, ln1_b,
               w_up, b_up, w_conv, b_conv, w_down, b_down, ln2_g, ln2_b)
    depth = w_in.shape[0]
    alpha = (2.0 * depth) ** 0.25
    layers = [_pack_layer(*(w[l] for w in weights)) for l in range(depth)]

    def trunk(x):
        batch, seq, _ = x.shape
        x2d = x.reshape(batch * seq, D_MODEL)
        for p in layers:
            x2d = _layer(x2d, p, batch, seq, alpha)
        return x2d.reshape(batch, seq, D_MODEL)

    return trunk(x_prompt), trunk(x_sample)
```

```python
import functools

import jax
import jax.numpy as jnp
from jax import lax
from jax.experimental import pallas as pl
from jax.experimental.pallas import tpu as pltpu

D_MODEL = 2048
D_MLSTM = 1024
N_HEADS = 4
HEAD_DIM = D_MLSTM // N_HEADS
D_POOL = D_MODEL - D_MLSTM
POOL_WINDOWS = (2, 4, 8, 16)
POOL_GROUP_DIM = D_POOL // len(POOL_WINDOWS)
D_FF = 5632
LN_EPS = 1e-5

LANES = 128
SUBLANES = 8
HALO = SUBLANES
CHUNK = 256
GATE_COLS = 2 * LANES
D_IN_PACKED = 4 * D_MLSTM + D_POOL + GATE_COLS
FF_TILE = 512
VMEM_LIMIT = 56 * 1024 * 1024

MXU_DTYPE = jnp.bfloat16
F32 = jnp.float32


def _mm(a, b):
    return jnp.dot(a, b, preferred_element_type=F32)


def _layer_norm_rows(y, g, b):
    mu = jnp.mean(y, axis=-1, keepdims=True)
    d = y - mu
    var = jnp.mean(d * d, axis=-1, keepdims=True)
    return d * lax.rsqrt(var + LN_EPS) * g + b


def _const_spec(shape):
    nd = len(shape)
    return pl.BlockSpec(shape, lambda *_: (0,) * nd, pipeline_mode=pl.Buffered(1))


def _inproj_kernel(x_ref, w_ref, b_ref, q_ref, k_ref, v_ref, o_ref, u_ref, gi_ref, gf_ref):
    xb = x_ref[...].astype(MXU_DTYPE)

    def proj(c0, c1):
        return _mm(xb, w_ref[:, c0:c1]) + b_ref[:, c0:c1]

    d = D_MLSTM
    q_ref[...] = proj(0, d).astype(q_ref.dtype)
    k_ref[...] = (proj(d, 2 * d) * (HEAD_DIM ** -0.5)).astype(k_ref.dtype)
    v_ref[...] = proj(2 * d, 3 * d).astype(v_ref.dtype)
    o_ref[...] = proj(3 * d, 4 * d)
    u_ref[...] = proj(4 * d, 4 * d + D_POOL)
    g = proj(4 * d + D_POOL, D_IN_PACKED)
    gi_ref[...] = g[:, :LANES]
    fp = g[:, LANES:]
    gf_ref[...] = jnp.minimum(fp, 0.0) - jnp.log1p(jnp.exp(-jnp.abs(fp)))


def _inproj(x2d, w_packed, b_packed, tm):
    n = x2d.shape[0]
    row = lambda w: pl.BlockSpec((tm, w), lambda i: (i, 0))
    out_shape = (
        jax.ShapeDtypeStruct((n, D_MLSTM), MXU_DTYPE),
        jax.ShapeDtypeStruct((n, D_MLSTM), MXU_DTYPE),
        jax.ShapeDtypeStruct((n, D_MLSTM), MXU_DTYPE),
        jax.ShapeDtypeStruct((n, D_MLSTM), F32),
        jax.ShapeDtypeStruct((n, D_POOL), F32),
        jax.ShapeDtypeStruct((n, LANES), F32),
        jax.ShapeDtypeStruct((n, LANES), F32),
    )
    return pl.pallas_call(
        _inproj_kernel,
        grid=(n // tm,),
        in_specs=[row(D_MODEL), _const_spec((D_MODEL, D_IN_PACKED)), _const_spec((1, D_IN_PACKED))],
        out_specs=(row(D_MLSTM), row(D_MLSTM), row(D_MLSTM), row(D_MLSTM), row(D_POOL),
                   row(LANES), row(LANES)),
        out_shape=out_shape,
        compiler_params=pltpu.CompilerParams(
            dimension_semantics=("parallel",), vmem_limit_bytes=VMEM_LIMIT),
        name="inproj",
    )(x2d, w_packed, b_packed)


def _scan_rows(x, reverse, combine, identity):
    n = x.shape[0]
    ridx = lax.broadcasted_iota(jnp.int32, x.shape, 0)
    sh = 1
    while sh < n:
        if reverse:
            shifted = pltpu.roll(x, n - sh, axis=0)
            valid = ridx < n - sh
        else:
            shifted = pltpu.roll(x, sh, axis=0)
            valid = ridx >= sh
        x = combine(x, jnp.where(valid, shifted, identity))
        sh *= 2
    return x


def _mlstm_kernel(qf_ref, kf_ref, vf_ref, gif_ref, gff_ref,
                  qb_ref, kb_ref, vb_ref, gib_ref, gfb_ref,
                  hf_ref, hb_ref, c_ref, m_ref):
    L = CHUNK

    @pl.when(pl.program_id(1) == 0)
    def _():
        c_ref[...] = jnp.zeros_like(c_ref)
        m_ref[...] = jnp.zeros_like(m_ref)

    t_idx = lax.broadcasted_iota(jnp.int32, (L, L), 0)
    s_idx = lax.broadcasted_iota(jnp.int32, (L, L), 1)
    m_prev = m_ref[0:1, :]
    ones_ext = jnp.ones((L, LANES), MXU_DTYPE)
    m_new_rows = []

    for direction, (q_ref, k_ref, v_ref, gi_ref, gf_ref, h_ref) in enumerate((
            (qf_ref, kf_ref, vf_ref, gif_ref, gff_ref, hf_ref),
            (qb_ref, kb_ref, vb_ref, gib_ref, gfb_ref, hb_ref))):
        reverse = direction == 1
        mask = (s_idx >= t_idx) if reverse else (s_idx <= t_idx)
        last = 0 if reverse else L - 1
        b = _scan_rows(gf_ref[...], reverse, jnp.add, 0.0)
        a = gi_ref[...] - b
        big_m = jnp.maximum(_scan_rows(a, reverse, jnp.maximum, -jnp.inf), m_prev)
        w_inter = jnp.exp(m_prev - big_m)
        exp_neg_m = jnp.exp(-(b + big_m))
        m_last = big_m[last:last + 1, :]
        decay = jnp.exp(m_prev - m_last)
        w_state = jnp.exp(a - m_last)
        m_new_rows.append(b[last:last + 1, :] + m_last)
        a_t = a.T

        for h in range(N_HEADS):
            lane = direction * N_HEADS + h
            cols = slice(h * HEAD_DIM, (h + 1) * HEAD_DIM)
            q = q_ref[:, cols]
            k = k_ref[:, cols]
            v_ext = jnp.concatenate([v_ref[:, cols], ones_ext], axis=1)
            s_qk = lax.dot_general(q, k, (((1,), (1,)), ((), ())), preferred_element_type=F32)
            logw = jnp.where(mask, a_t[lane:lane + 1, :] - big_m[:, lane:lane + 1], -jnp.inf)
            s = (s_qk * jnp.exp(logw)).astype(MXU_DTYPE)
            c_old = c_ref[lane]
            r = _mm(s, v_ext) + w_inter[:, lane:lane + 1] * _mm(q, c_old.astype(MXU_DTYPE))
            den = jnp.maximum(jnp.abs(r[:, HEAD_DIM:]), exp_neg_m[:, lane:lane + 1])
            inv = 1.0 / den
            h_ref[:, cols] = r[:, :HEAD_DIM] * jnp.concatenate([inv] * (HEAD_DIM // LANES), axis=1)
            kw = (k.astype(F32) * w_state[:, lane:lane + 1]).astype(MXU_DTYPE)
            upd = lax.dot_general(kw, v_ext, (((0,), (0,)), ((), ())), preferred_element_type=F32)
            c_ref[lane] = decay[:, lane:lane + 1] * c_old + upd

    lane_idx = lax.broadcasted_iota(jnp.int32, (1, LANES), 1)
    m_ref[0:1, :] = jnp.where(lane_idx < N_HEADS, m_new_rows[0], m_new_rows[1])


def _mlstm(q, k, v, gi, gf, batch, seq):
    n = q.shape[0]
    nc = seq // CHUNK
    fwd = lambda w: pl.BlockSpec((CHUNK, w), lambda b, c: (b * nc + c, 0))
    bwd = lambda w: pl.BlockSpec((CHUNK, w), lambda b, c: (b * nc + nc - 1 - c, 0))
    return pl.pallas_call(
        _mlstm_kernel,
        grid=(batch, nc),
        in_specs=[fwd(D_MLSTM), fwd(D_MLSTM), fwd(D_MLSTM), fwd(LANES), fwd(LANES),
                  bwd(D_MLSTM), bwd(D_MLSTM), bwd(D_MLSTM), bwd(LANES), bwd(LANES)],
        out_specs=(fwd(D_MLSTM), bwd(D_MLSTM)),
        out_shape=(jax.ShapeDtypeStruct((n, D_MLSTM), F32), jax.ShapeDtypeStruct((n, D_MLSTM), F32)),
        scratch_shapes=[pltpu.VMEM((2 * N_HEADS, HEAD_DIM, HEAD_DIM + LANES), F32),
                        pltpu.VMEM((SUBLANES, LANES), F32)],
        compiler_params=pltpu.CompilerParams(
            dimension_semantics=("parallel", "arbitrary"), vmem_limit_bytes=VMEM_LIMIT),
        name="mlstm_scan",
    )(q, k, v, gi, gf, q, k, v, gi, gf)


def _outproj_kernel(hf_ref, hb_ref, o_ref, u_ref, up_ref, un_ref, x_ref,
                    mhg_ref, wp_ref, ps_ref, wo_ref, bo_ref, g_ref, b_ref,
                    y_ref, uext_ref, *, tiles_per_seq, seq, alpha):
    tm = x_ref.shape[0]
    t_in_seq = pl.program_id(0) % tiles_per_seq
    not_first = (t_in_seq > 0).astype(F32)
    not_last = (t_in_seq < tiles_per_seq - 1).astype(F32)

    hs = hf_ref[...] + hb_ref[...]
    parts = []
    for h in range(N_HEADS):
        xh = hs[:, h * HEAD_DIM:(h + 1) * HEAD_DIM]
        mu = jnp.mean(xh, axis=-1, keepdims=True)
        d = xh - mu
        var = jnp.mean(d * d, axis=-1, keepdims=True)
        parts.append(d * lax.rsqrt(var + LN_EPS))
    hm = jnp.concatenate(parts, axis=1) * mhg_ref[...] * jax.nn.sigmoid(o_ref[...])

    u = u_ref[...]
    uext_ref[0:HALO, :] = up_ref[...] * not_first
    uext_ref[HALO:HALO + tm, :] = u
    uext_ref[HALO + tm:, :] = un_ref[...] * not_last
    pos = t_in_seq * tm + lax.broadcasted_iota(jnp.int32, (tm, 1), 0)
    hp_parts = []
    for gi, w in enumerate(POOL_WINDOWS):
        cols = slice(gi * POOL_GROUP_DIM, (gi + 1) * POOL_GROUP_DIM)
        acc = uext_ref[HALO - w // 2:HALO - w // 2 + tm, cols]
        for j in range(-(w // 2) + 1, w - w // 2):
            acc = acc + uext_ref[HALO + j:HALO + j + tm, cols]
        cnt = (jnp.minimum(pos + (w - w // 2), seq) - jnp.maximum(pos - w // 2, 0)).astype(F32)
        pooled = acc / cnt - u[:, cols]
        hp_parts.append(_mm(pooled.astype(MXU_DTYPE), wp_ref[gi]))
    hp = jnp.concatenate(hp_parts, axis=1) * ps_ref[...]

    mixed = jnp.concatenate([hm, hp], axis=1).astype(MXU_DTYPE)
    half = D_MODEL // 2
    mix = jnp.concatenate([_mm(mixed, wo_ref[:, :half]), _mm(mixed, wo_ref[:, half:])],
                          axis=1) + bo_ref[...]
    y_ref[...] = _layer_norm_rows(alpha * x_ref[...] + mix, g_ref[...], b_ref[...])


def _outproj(hf, hb, o, u, x2d, mh_g, w_pool, pool_scale, w_out, b_out, ln_g, ln_b, seq, tm, alpha):
    n = x2d.shape[0]
    tiles_per_seq = seq // tm
    hb_per_tile = tm // HALO
    n_halo_blocks = n // HALO
    row = lambda w: pl.BlockSpec((tm, w), lambda i: (i, 0))
    prev = pl.BlockSpec((HALO, D_POOL), lambda i: (jnp.maximum(i * hb_per_tile - 1, 0), 0))
    nxt = pl.BlockSpec((HALO, D_POOL),
                       lambda i: (jnp.minimum((i + 1) * hb_per_tile, n_halo_blocks - 1), 0))
    kern = functools.partial(_outproj_kernel, tiles_per_seq=tiles_per_seq, seq=seq, alpha=alpha)
    return pl.pallas_call(
        kern,
        grid=(n // tm,),
        in_specs=[row(D_MLSTM), row(D_MLSTM), row(D_MLSTM), row(D_POOL), prev, nxt, row(D_MODEL),
                  _const_spec((1, D_MLSTM)),
                  _const_spec((len(POOL_WINDOWS), POOL_GROUP_DIM, POOL_GROUP_DIM)),
                  _const_spec((1, D_POOL)), _const_spec((D_MODEL, D_MODEL)),
                  _const_spec((1, D_MODEL)), _const_spec((1, D_MODEL)), _const_spec((1, D_MODEL))],
        out_specs=row(D_MODEL),
        out_shape=jax.ShapeDtypeStruct((n, D_MODEL), F32),
        scratch_shapes=[pltpu.VMEM((tm + 2 * HALO, D_POOL), F32)],
        compiler_params=pltpu.CompilerParams(
            dimension_semantics=("parallel",), vmem_limit_bytes=VMEM_LIMIT),
        name="outproj_ln",
    )(hf, hb, o, u, u, u, x2d, mh_g, w_pool, pool_scale, w_out, b_out, ln_g, ln_b)


def _ffn_kernel(x_ref, xp_ref, xn_ref, wg_ref, wv_ref, vg_ref, vv_ref, wd_ref, bd_ref, g_ref, b_ref,
                y_ref, xs_ref, a_ref, *, tiles_per_seq, alpha):
    tm = x_ref.shape[0]
    tf = wd_ref.shape[0]
    j = pl.program_id(1)
    t_in_seq = pl.program_id(0) % tiles_per_seq
    not_first = (t_in_seq > 0).astype(F32)
    not_last = (t_in_seq < tiles_per_seq - 1).astype(F32)

    @pl.when(j == 0)
    def _():
        x = x_ref[...]
        xs_ref[...] = jnp.concatenate([xp_ref[...], x, xn_ref[...]], axis=0).astype(xs_ref.dtype)
        y_ref[...] = alpha * x + bd_ref[...]

    xs = xs_ref[...]
    conv = []
    for part, (w_ref, vec_ref) in enumerate(((wg_ref, vg_ref), (wv_ref, vv_ref))):
        cols = slice(part * tf, (part + 1) * tf)
        a = _mm(xs, w_ref[...]) + vec_ref[0:1, :]
        a_ref[0:HALO, cols] = a[0:HALO] * not_first
        a_ref[HALO:HALO + tm, cols] = a[HALO:HALO + tm]
        a_ref[HALO + tm:, cols] = a[HALO + tm:] * not_last
        conv.append(vec_ref[1:2, :] * a_ref[HALO - 1:HALO - 1 + tm, cols]
                    + vec_ref[2:3, :] * a_ref[HALO:HALO + tm, cols]
                    + vec_ref[3:4, :] * a_ref[HALO + 1:HALO + 1 + tm, cols] + vec_ref[4:5, :])
    gate, val = conv
    f = (gate * jax.nn.sigmoid(gate) * val).astype(MXU_DTYPE)
    for cols in (slice(0, D_MODEL // 2), slice(D_MODEL // 2, D_MODEL)):
        y_ref[:, cols] += _mm(f, wd_ref[:, cols])

    @pl.when(j == pl.num_programs(1) - 1)
    def _():
        y_ref[...] = _layer_norm_rows(y_ref[...], g_ref[...], b_ref[...])


def _ffn(x1, w_up, ff_vecs, w_down, b_down, ln_g, ln_b, seq, tm, tf, alpha):
    n = x1.shape[0]
    tiles_per_seq = seq // tm
    hb_per_tile = tm // HALO
    n_halo_blocks = n // HALO
    nj = D_FF // tf
    main = pl.BlockSpec((tm, D_MODEL), lambda i, j: (i, 0))
    prev = pl.BlockSpec((HALO, D_MODEL), lambda i, j: (jnp.maximum(i * hb_per_tile - 1, 0), 0))
    nxt = pl.BlockSpec((HALO, D_MODEL),
                       lambda i, j: (jnp.minimum((i + 1) * hb_per_tile, n_halo_blocks - 1), 0))
    up_gate = pl.BlockSpec((D_MODEL, tf), lambda i, j: (0, j))
    up_val = pl.BlockSpec((D_MODEL, tf), lambda i, j: (0, j + nj))
    vec_gate = pl.BlockSpec((SUBLANES, tf), lambda i, j: (0, j))
    vec_val = pl.BlockSpec((SUBLANES, tf), lambda i, j: (0, j + nj))
    down = pl.BlockSpec((tf, D_MODEL), lambda i, j: (j, 0))
    vec = pl.BlockSpec((1, D_MODEL), lambda i, j: (0, 0))
    kern = functools.partial(_ffn_kernel, tiles_per_seq=tiles_per_seq, alpha=alpha)
    return pl.pallas_call(
        kern,
        grid=(n // tm, nj),
        in_specs=[main, prev, nxt, up_gate, up_val, vec_gate, vec_val, down, vec, vec, vec],
        out_specs=main,
        out_shape=jax.ShapeDtypeStruct((n, D_MODEL), F32),
        scratch_shapes=[pltpu.VMEM((tm + 2 * HALO, D_MODEL), MXU_DTYPE),
                        pltpu.VMEM((tm + 2 * HALO, 2 * tf), F32)],
        compiler_params=pltpu.CompilerParams(
            dimension_semantics=("parallel", "arbitrary"), vmem_limit_bytes=VMEM_LIMIT),
        name="ffn_ln",
    )(x1, x1, x1, w_up, w_up, ff_vecs, ff_vecs, w_down, b_down, ln_g, ln_b)


def _pack_layer(w_in, b_in, mh_norm_g, w_pool, pool_scale, w_out, b_out, ln1_g, ln1_b,
                w_up, b_up, w_conv, b_conv, w_down, b_down, ln2_g, ln2_b):
    g0 = 4 * D_MLSTM
    h = N_HEADS
    ng = 4 * h

    def gate_cols(a):
        gts = a[..., g0:g0 + ng]
        pad = jnp.zeros(a.shape[:-1] + (LANES - 2 * h,), a.dtype)
        return jnp.concatenate([gts[..., 0:h], gts[..., 2 * h:3 * h], pad,
                                gts[..., h:2 * h], gts[..., 3 * h:4 * h], pad], axis=-1)

    pack_in = lambda a: jnp.concatenate([a[..., :g0], a[..., g0 + ng:], gate_cols(a)], axis=-1)
    row = lambda a: a.reshape(1, -1).astype(F32)

    return dict(
        w_in=pack_in(w_in).astype(MXU_DTYPE), b_in=row(pack_in(b_in)),
        mh_g=row(mh_norm_g), w_pool=w_pool.astype(MXU_DTYPE), pool_scale=row(pool_scale),
        w_out=w_out.astype(MXU_DTYPE), b_out=row(b_out), ln1_g=row(ln1_g), ln1_b=row(ln1_b),
        w_up=w_up.astype(MXU_DTYPE),
        ff_vecs=jnp.concatenate(
            [b_up[None], w_conv, b_conv[None], jnp.zeros((SUBLANES - 5, 2 * D_FF), F32)],
            axis=0).astype(F32),
        w_down=w_down.astype(MXU_DTYPE), b_down=row(b_down), ln2_g=row(ln2_g), ln2_b=row(ln2_b),
    )


def _tile(seq, want):
    return min(want, seq)


def _layer(x2d, p, batch, seq, alpha):
    q, k, v, o, u, gi, gf = _inproj(x2d, p["w_in"], p["b_in"], _tile(seq, 512))
    hf, hb = _mlstm(q, k, v, gi, gf, batch, seq)
    x1 = _outproj(hf, hb, o, u, x2d, p["mh_g"], p["w_pool"], p["pool_scale"], p["w_out"], p["b_out"],
                  p["ln1_g"], p["ln1_b"], seq, _tile(seq, 512), alpha)
    return _ffn(x1, p["w_up"], p["ff_vecs"], p["w_down"], p["b_down"], p["ln2_g"], p["ln2_b"],
                seq, _tile(seq, 512), FF_TILE, alpha)


def kernel(x_prompt, x_sample, w_in, b_in, mh_norm_g, w_pool, pool_scale, w_out, b_out, ln1_g, ln1_b,
           w_up, b_up, w_conv, b_conv, w_down, b_down, ln2_g, ln2_b):
    weights = (w_in, b_in, mh_norm_g, w_pool, pool_scale, w_out, b_out, ln1_g, ln1_b,
               w_up, b_up, w_conv, b_conv, w_down, b_down, ln2_g, ln2_b)
    depth = w_in.shape[0]
    alpha = (2.0 * depth) ** 0.25
    layers = [_pack_layer(*(w[l] for w in weights)) for l in range(depth)]

    def trunk(x):
        batch, seq, _ = x.shape
        x2d = x.reshape(batch * seq, D_MODEL)
        for p in layers:
            x2d = _layer(x2d, p, batch, seq, alpha)
        return x2d.reshape(batch, seq, D_MODEL)

    return trunk(x_prompt), trunk(x_sample)
```

```python
import functools

import jax
import jax.numpy as jnp
from jax import lax
from jax.experimental import pallas as pl
from jax.experimental.pallas import tpu as pltpu

D_MODEL = 2048
D_MLSTM = 1024
N_HEADS = 4
HEAD_DIM = D_MLSTM // N_HEADS
D_POOL = D_MODEL - D_MLSTM
POOL_WINDOWS = (2, 4, 8, 16)
POOL_GROUP_DIM = D_POOL // len(POOL_WINDOWS)
D_FF = 5632
LN_EPS = 1e-5

LANES = 128
SUBLANES = 8
HALO = SUBLANES
CHUNK = 256
GATE_COLS = 2 * LANES
D_IN_PACKED = 4 * D_MLSTM + D_POOL + GATE_COLS
FF_SUB = 256
FF_TILE = 2 * FF_SUB
ROW_CHUNK = 128
VMEM_LIMIT = 56 * 1024 * 1024

MXU_DTYPE = jnp.bfloat16
F32 = jnp.float32


def _mm(a, b):
    return jnp.dot(a, b, preferred_element_type=F32)


def _layer_norm_rows(y, g, b):
    mu = jnp.mean(y, axis=-1, keepdims=True)
    d = y - mu
    var = jnp.mean(d * d, axis=-1, keepdims=True)
    return d * lax.rsqrt(var + LN_EPS) * g + b


def _const_spec(shape):
    nd = len(shape)
    return pl.BlockSpec(shape, lambda *_: (0,) * nd, pipeline_mode=pl.Buffered(1))


def _inproj_kernel(x_ref, w_ref, wu_ref, wg_ref, b_ref,
                   q_ref, k_ref, v_ref, o_ref, u_ref, gi_ref, gf_ref):
    xb = x_ref[...].astype(MXU_DTYPE)
    d = D_MLSTM

    def proj(c0, c1):
        return _mm(xb, w_ref[:, c0:c1]) + b_ref[:, c0:c1]

    q_ref[...] = proj(0, d).astype(q_ref.dtype)
    k_ref[...] = (proj(d, 2 * d) * (HEAD_DIM ** -0.5)).astype(k_ref.dtype)
    v_ref[...] = proj(2 * d, 3 * d).astype(v_ref.dtype)
    o_ref[...] = proj(3 * d, 4 * d)
    u_ref[...] = _mm(xb, wu_ref[...]) + b_ref[:, 4 * d:4 * d + D_POOL]
    g = _mm(xb, wg_ref[...]) + b_ref[:, 4 * d + D_POOL:]
    gi_ref[...] = g[:, :LANES]
    fp = g[:, LANES:]
    gf_ref[...] = jnp.minimum(fp, 0.0) - jnp.log1p(jnp.exp(-jnp.abs(fp)))


def _inproj(x2d, w_qkvo, w_u, w_gates, b_packed, tm):
    n = x2d.shape[0]
    row = lambda w: pl.BlockSpec((tm, w), lambda i: (i, 0))
    out_shape = (
        jax.ShapeDtypeStruct((n, D_MLSTM), MXU_DTYPE),
        jax.ShapeDtypeStruct((n, D_MLSTM), MXU_DTYPE),
        jax.ShapeDtypeStruct((n, D_MLSTM), MXU_DTYPE),
        jax.ShapeDtypeStruct((n, D_MLSTM), F32),
        jax.ShapeDtypeStruct((n, D_POOL), F32),
        jax.ShapeDtypeStruct((n, LANES), F32),
        jax.ShapeDtypeStruct((n, LANES), F32),
    )
    return pl.pallas_call(
        _inproj_kernel,
        grid=(n // tm,),
        in_specs=[row(D_MODEL), _const_spec((D_MODEL, 4 * D_MLSTM)), _const_spec((D_MODEL, D_POOL)),
                  _const_spec((D_MODEL, GATE_COLS)), _const_spec((1, D_IN_PACKED))],
        out_specs=(row(D_MLSTM), row(D_MLSTM), row(D_MLSTM), row(D_MLSTM), row(D_POOL),
                   row(LANES), row(LANES)),
        out_shape=out_shape,
        compiler_params=pltpu.CompilerParams(
            dimension_semantics=("parallel",), vmem_limit_bytes=VMEM_LIMIT),
        name="inproj",
    )(x2d, w_qkvo, w_u, w_gates, b_packed)


def _scan_rows(x, reverse, combine, identity):
    n = x.shape[0]
    ridx = lax.broadcasted_iota(jnp.int32, x.shape, 0)
    sh = 1
    while sh < n:
        if reverse:
            shifted = pltpu.roll(x, n - sh, axis=0)
            valid = ridx < n - sh
        else:
            shifted = pltpu.roll(x, sh, axis=0)
            valid = ridx >= sh
        x = combine(x, jnp.where(valid, shifted, identity))
        sh *= 2
    return x


def _mlstm_kernel(qf_ref, kf_ref, vf_ref, gif_ref, gff_ref,
                  qb_ref, kb_ref, vb_ref, gib_ref, gfb_ref,
                  hf_ref, hb_ref, c_ref, m_ref):
    L = CHUNK

    @pl.when(pl.program_id(1) == 0)
    def _():
        c_ref[...] = jnp.zeros_like(c_ref)
        m_ref[...] = jnp.zeros_like(m_ref)

    t_idx = lax.broadcasted_iota(jnp.int32, (L, L), 0)
    s_idx = lax.broadcasted_iota(jnp.int32, (L, L), 1)
    m_prev = m_ref[0:1, :]
    ones_ext = jnp.ones((L, LANES), MXU_DTYPE)
    m_new_rows = []

    for direction, (q_ref, k_ref, v_ref, gi_ref, gf_ref, h_ref) in enumerate((
            (qf_ref, kf_ref, vf_ref, gif_ref, gff_ref, hf_ref),
            (qb_ref, kb_ref, vb_ref, gib_ref, gfb_ref, hb_ref))):
        reverse = direction == 1
        mask = (s_idx >= t_idx) if reverse else (s_idx <= t_idx)
        last = 0 if reverse else L - 1
        b = _scan_rows(gf_ref[...], reverse, jnp.add, 0.0)
        a = gi_ref[...] - b
        big_m = jnp.maximum(_scan_rows(a, reverse, jnp.maximum, -jnp.inf), m_prev)
        w_inter = jnp.exp(m_prev - big_m)
        exp_neg_m = jnp.exp(-(b + big_m))
        m_last = big_m[last:last + 1, :]
        decay = jnp.exp(m_prev - m_last)
        w_state = jnp.exp(a - m_last)
        m_new_rows.append(b[last:last + 1, :] + m_last)
        a_t = a.T

        for h in range(N_HEADS):
            lane = direction * N_HEADS + h
            cols = slice(h * HEAD_DIM, (h + 1) * HEAD_DIM)
            q = q_ref[:, cols]
            k = k_ref[:, cols]
            v_ext = jnp.concatenate([v_ref[:, cols], ones_ext], axis=1)
            s_qk = lax.dot_general(q, k, (((1,), (1,)), ((), ())), preferred_element_type=F32)
            logw = jnp.where(mask, a_t[lane:lane + 1, :] - big_m[:, lane:lane + 1], -jnp.inf)
            s = (s_qk * jnp.exp(logw)).astype(MXU_DTYPE)
            c_old = c_ref[lane]
            r = _mm(s, v_ext) + w_inter[:, lane:lane + 1] * _mm(q, c_old.astype(MXU_DTYPE))
            den = jnp.maximum(jnp.abs(r[:, HEAD_DIM:]), exp_neg_m[:, lane:lane + 1])
            inv = 1.0 / den
            h_ref[:, cols] = r[:, :HEAD_DIM] * jnp.concatenate([inv] * (HEAD_DIM // LANES), axis=1)
            kw = (k.astype(F32) * w_state[:, lane:lane + 1]).astype(MXU_DTYPE)
            upd = lax.dot_general(kw, v_ext, (((0,), (0,)), ((), ())), preferred_element_type=F32)
            c_ref[lane] = decay[:, lane:lane + 1] * c_old + upd

    lane_idx = lax.broadcasted_iota(jnp.int32, (1, LANES), 1)
    m_ref[0:1, :] = jnp.where(lane_idx < N_HEADS, m_new_rows[0], m_new_rows[1])


def _mlstm(q, k, v, gi, gf, batch, seq):
    n = q.shape[0]
    nc = seq // CHUNK
    fwd = lambda w: pl.BlockSpec((CHUNK, w), lambda b, c: (b * nc + c, 0))
    bwd = lambda w: pl.BlockSpec((CHUNK, w), lambda b, c: (b * nc + nc - 1 - c, 0))
    return pl.pallas_call(
        _mlstm_kernel,
        grid=(batch, nc),
        in_specs=[fwd(D_MLSTM), fwd(D_MLSTM), fwd(D_MLSTM), fwd(LANES), fwd(LANES),
                  bwd(D_MLSTM), bwd(D_MLSTM), bwd(D_MLSTM), bwd(LANES), bwd(LANES)],
        out_specs=(fwd(D_MLSTM), bwd(D_MLSTM)),
        out_shape=(jax.ShapeDtypeStruct((n, D_MLSTM), F32), jax.ShapeDtypeStruct((n, D_MLSTM), F32)),
        scratch_shapes=[pltpu.VMEM((2 * N_HEADS, HEAD_DIM, HEAD_DIM + LANES), F32),
                        pltpu.VMEM((SUBLANES, LANES), F32)],
        compiler_params=pltpu.CompilerParams(
            dimension_semantics=("parallel", "arbitrary"), vmem_limit_bytes=VMEM_LIMIT),
        name="mlstm_scan",
    )(q, k, v, gi, gf, q, k, v, gi, gf)


def _outproj_kernel(hf_ref, hb_ref, o_ref, u_ref, up_ref, un_ref, x_ref,
                    mhg_ref, wp_ref, ps_ref, wo_ref, bo_ref, g_ref, b_ref,
                    y_ref, uext_ref, *, tiles_per_seq, seq, alpha):
    tm = x_ref.shape[0]
    t_in_seq = pl.program_id(0) % tiles_per_seq
    not_first = (t_in_seq > 0).astype(F32)
    not_last = (t_in_seq < tiles_per_seq - 1).astype(F32)

    uext_ref[0:HALO, :] = up_ref[...] * not_first
    uext_ref[HALO:HALO + tm, :] = u_ref[...]
    uext_ref[HALO + tm:, :] = un_ref[...] * not_last

    for r0 in range(0, tm, ROW_CHUNK):
        rows = slice(r0, r0 + ROW_CHUNK)
        hs = hf_ref[rows, :] + hb_ref[rows, :]
        parts = []
        for h in range(N_HEADS):
            xh = hs[:, h * HEAD_DIM:(h + 1) * HEAD_DIM]
            mu = jnp.mean(xh, axis=-1, keepdims=True)
            d = xh - mu
            var = jnp.mean(d * d, axis=-1, keepdims=True)
            parts.append(d * lax.rsqrt(var + LN_EPS))
        hm = jnp.concatenate(parts, axis=1) * mhg_ref[...] * jax.nn.sigmoid(o_ref[rows, :])

        pos = t_in_seq * tm + r0 + lax.broadcasted_iota(jnp.int32, (ROW_CHUNK, 1), 0)
        base = HALO + r0
        hp_parts = []
        for gi, w in enumerate(POOL_WINDOWS):
            cols = slice(gi * POOL_GROUP_DIM, (gi + 1) * POOL_GROUP_DIM)
            acc = uext_ref[base - w // 2:base - w // 2 + ROW_CHUNK, cols]
            for j in range(-(w // 2) + 1, w - w // 2):
                acc = acc + uext_ref[base + j:base + j + ROW_CHUNK, cols]
            cnt = (jnp.minimum(pos + (w - w // 2), seq) - jnp.maximum(pos - w // 2, 0)).astype(F32)
            pooled = acc / cnt - uext_ref[base:base + ROW_CHUNK, cols]
            hp_parts.append(_mm(pooled.astype(MXU_DTYPE), wp_ref[gi]))
        hp = jnp.concatenate(hp_parts, axis=1) * ps_ref[...]

        mixed = jnp.concatenate([hm, hp], axis=1).astype(MXU_DTYPE)
        mix = _mm(mixed, wo_ref[...]) + bo_ref[...]
        y_ref[rows, :] = _layer_norm_rows(alpha * x_ref[rows, :] + mix, g_ref[...], b_ref[...])


def _outproj(hf, hb, o, u, x2d, mh_g, w_pool, pool_scale, w_out, b_out, ln_g, ln_b, seq, tm, alpha):
    n = x2d.shape[0]
    tiles_per_seq = seq // tm
    hb_per_tile = tm // HALO
    n_halo_blocks = n // HALO
    row = lambda w: pl.BlockSpec((tm, w), lambda i: (i, 0))
    prev = pl.BlockSpec((HALO, D_POOL), lambda i: (jnp.maximum(i * hb_per_tile - 1, 0), 0))
    nxt = pl.BlockSpec((HALO, D_POOL),
                       lambda i: (jnp.minimum((i + 1) * hb_per_tile, n_halo_blocks - 1), 0))
    kern = functools.partial(_outproj_kernel, tiles_per_seq=tiles_per_seq, seq=seq, alpha=alpha)
    return pl.pallas_call(
        kern,
        grid=(n // tm,),
        in_specs=[row(D_MLSTM), row(D_MLSTM), row(D_MLSTM), row(D_POOL), prev, nxt, row(D_MODEL),
                  _const_spec((1, D_MLSTM)),
                  _const_spec((len(POOL_WINDOWS), POOL_GROUP_DIM, POOL_GROUP_DIM)),
                  _const_spec((1, D_POOL)), _const_spec((D_MODEL, D_MODEL)),
                  _const_spec((1, D_MODEL)), _const_spec((1, D_MODEL)), _const_spec((1, D_MODEL))],
        out_specs=row(D_MODEL),
        out_shape=jax.ShapeDtypeStruct((n, D_MODEL), F32),
        scratch_shapes=[pltpu.VMEM((tm + 2 * HALO, D_POOL), F32)],
        compiler_params=pltpu.CompilerParams(
            dimension_semantics=("parallel",), vmem_limit_bytes=VMEM_LIMIT),
        name="outproj_ln",
    )(hf, hb, o, u, u, u, x2d, mh_g, w_pool, pool_scale, w_out, b_out, ln_g, ln_b)


def _ffn_kernel(x_ref, xp_ref, xn_ref, wg_ref, wv_ref, vg_ref, vv_ref, wd_ref, bd_ref, g_ref, b_ref,
                y_ref, xs_ref, a0_ref, a1_ref, *, tiles_per_seq, alpha):
    tm = x_ref.shape[0]
    j = pl.program_id(1)
    t_in_seq = pl.program_id(0) % tiles_per_seq
    not_first = (t_in_seq > 0).astype(F32)
    not_last = (t_in_seq < tiles_per_seq - 1).astype(F32)

    @pl.when(j == 0)
    def _():
        x = x_ref[...]
        xs_ref[...] = jnp.concatenate([xp_ref[...], x, xn_ref[...]], axis=0).astype(xs_ref.dtype)
        y_ref[...] = alpha * x + bd_ref[...]

    xs = xs_ref[...]
    fs = []
    for sub, a_ref in enumerate((a0_ref, a1_ref)):
        src = slice(sub * FF_SUB, (sub + 1) * FF_SUB)
        for part, (w_ref, vec_ref) in enumerate(((wg_ref, vg_ref), (wv_ref, vv_ref))):
            dst = slice(part * FF_SUB, (part + 1) * FF_SUB)
            a = _mm(xs, w_ref[:, src]) + vec_ref[0:1, src]
            a_ref[0:HALO, dst] = a[0:HALO] * not_first
            a_ref[HALO:HALO + tm, dst] = a[HALO:HALO + tm]
            a_ref[HALO + tm:, dst] = a[HALO + tm:] * not_last
        conv = []
        for part, vec_ref in enumerate((vg_ref, vv_ref)):
            dst = slice(part * FF_SUB, (part + 1) * FF_SUB)
            conv.append(vec_ref[1:2, src] * a_ref[HALO - 1:HALO - 1 + tm, dst]
                        + vec_ref[2:3, src] * a_ref[HALO:HALO + tm, dst]
                        + vec_ref[3:4, src] * a_ref[HALO + 1:HALO + 1 + tm, dst] + vec_ref[4:5, src])
        gate, val = conv
        fs.append((gate * jax.nn.sigmoid(gate) * val).astype(MXU_DTYPE))
    for sub in range(len(fs)):
        for cols in (slice(0, D_MODEL // 2), slice(D_MODEL // 2, D_MODEL)):
            y_ref[:, cols] += _mm(fs[sub], wd_ref[sub * FF_SUB:(sub + 1) * FF_SUB, cols])

    @pl.when(j == pl.num_programs(1) - 1)
    def _():
        y_ref[...] = _layer_norm_rows(y_ref[...], g_ref[...], b_ref[...])


def _ffn(x1, w_up, ff_vecs, w_down, b_down, ln_g, ln_b, seq, tm, tf, alpha):
    n = x1.shape[0]
    tiles_per_seq = seq // tm
    hb_per_tile = tm // HALO
    n_halo_blocks = n // HALO
    nj = D_FF // tf
    main = pl.BlockSpec((tm, D_MODEL), lambda i, j: (i, 0))
    prev = pl.BlockSpec((HALO, D_MODEL), lambda i, j: (jnp.maximum(i * hb_per_tile - 1, 0), 0))
    nxt = pl.BlockSpec((HALO, D_MODEL),
                       lambda i, j: (jnp.minimum((i + 1) * hb_per_tile, n_halo_blocks - 1), 0))
    up_gate = pl.BlockSpec((D_MODEL, tf), lambda i, j: (0, j))
    up_val = pl.BlockSpec((D_MODEL, tf), lambda i, j: (0, j + nj))
    vec_gate = pl.BlockSpec((SUBLANES, tf), lambda i, j: (0, j))
    vec_val = pl.BlockSpec((SUBLANES, tf), lambda i, j: (0, j + nj))
    down = pl.BlockSpec((tf, D_MODEL), lambda i, j: (j, 0))
    vec = pl.BlockSpec((1, D_MODEL), lambda i, j: (0, 0))
    kern = functools.partial(_ffn_kernel, tiles_per_seq=tiles_per_seq, alpha=alpha)
    return pl.pallas_call(
        kern,
        grid=(n // tm, nj),
        in_specs=[main, prev, nxt, up_gate, up_val, vec_gate, vec_val, down, vec, vec, vec],
        out_specs=main,
        out_shape=jax.ShapeDtypeStruct((n, D_MODEL), F32),
        scratch_shapes=[pltpu.VMEM((tm + 2 * HALO, D_MODEL), MXU_DTYPE),
                        pltpu.VMEM((tm + 2 * HALO, 2 * FF_SUB), F32),
                        pltpu.VMEM((tm + 2 * HALO, 2 * FF_SUB), F32)],
        compiler_params=pltpu.CompilerParams(
            dimension_semantics=("parallel", "arbitrary"), vmem_limit_bytes=VMEM_LIMIT),
        name="ffn_ln",
    )(x1, x1, x1, w_up, w_up, ff_vecs, ff_vecs, w_down, b_down, ln_g, ln_b)


def _pack_layer(w_in, b_in, mh_norm_g, w_pool, pool_scale, w_out, b_out, ln1_g, ln1_b,
                w_up, b_up, w_conv, b_conv, w_down, b_down, ln2_g, ln2_b):
    g0 = 4 * D_MLSTM
    h = N_HEADS
    ng = 4 * h

    def gate_cols(a):
        gts = a[..., g0:g0 + ng]
        pad = jnp.zeros(a.shape[:-1] + (LANES - 2 * h,), a.dtype)
        return jnp.concatenate([gts[..., 0:h], gts[..., 2 * h:3 * h], pad,
                                gts[..., h:2 * h], gts[..., 3 * h:4 * h], pad], axis=-1)

    pack_in = lambda a: jnp.concatenate([a[..., :g0], a[..., g0 + ng:], gate_cols(a)], axis=-1)
    row = lambda a: a.reshape(1, -1).astype(F32)

    return dict(
        w_qkvo=w_in[:, :g0].astype(MXU_DTYPE), w_u=w_in[:, g0 + ng:].astype(MXU_DTYPE),
        w_gates=gate_cols(w_in).astype(MXU_DTYPE), b_in=row(pack_in(b_in)),
        mh_g=row(mh_norm_g), w_pool=w_pool.astype(MXU_DTYPE), pool_scale=row(pool_scale),
        w_out=w_out.astype(MXU_DTYPE), b_out=row(b_out), ln1_g=row(ln1_g), ln1_b=row(ln1_b),
        w_up=w_up.astype(MXU_DTYPE),
        ff_vecs=jnp.concatenate(
            [b_up[None], w_conv, b_conv[None], jnp.zeros((SUBLANES - 5, 2 * D_FF), F32)],
            axis=0).astype(F32),
        w_down=w_down.astype(MXU_DTYPE), b_down=row(b_down), ln2_g=row(ln2_g), ln2_b=row(ln2_b),
    )


def _tile(seq, want):
    return min(want, seq)


def _layer(x2d, p, batch, seq, alpha):
    q, k, v, o, u, gi, gf = _inproj(x2d, p["w_qkvo"], p["w_u"], p["w_gates"], p["b_in"],
                                    _tile(seq, 512))
    hf, hb = _mlstm(q, k, v, gi, gf, batch, seq)
    x1 = _outproj(hf, hb, o, u, x2d, p["mh_g"], p["w_pool"], p["pool_scale"], p["w_out"], p["b_out"],
                  p["ln1_g"], p["ln1_b"], seq, _tile(seq, 512), alpha)
    return _ffn(x1, p["w_up"], p["ff_vecs"], p["w_down"], p["b_down"], p["ln2_g"], p["ln2_b"],
                seq, _tile(seq, 512), FF_TILE, alpha)


def kernel(x_prompt, x_sample, w_in, b_in, mh_norm_g, w_pool, pool_scale, w_out, b_out, ln1_g, ln1_b,
           w_up, b_up, w_conv, b_conv, w_down, b_down, ln2_g, ln2_b):
    weights = (w_in, b_in, mh_norm_g, w_pool, pool_scale, w_out, b_out, ln1_g, ln1_b,
               w_up, b_up, w_conv, b_conv, w_down, b_down, ln2_g, ln2_b)
    depth = w_in.shape[0]
    alpha = (2.0 * depth) ** 0.25
    layers = [_pack_layer(*(w[l] for w in weights)) for l in range(depth)]

    def trunk(x):
        batch, seq, _ = x.shape
        x2d = x.reshape(batch * seq, D_MODEL)
        for p in layers:
            x2d = _layer(x2d, p, batch, seq, alpha)
        return x2d.reshape(batch, seq, D_MODEL)

    return trunk(x_prompt), trunk(x_sample)
```

```python
import functools

import jax
import jax.numpy as jnp
from jax import lax
from jax.experimental import pallas as pl
from jax.experimental.pallas import tpu as pltpu

D_MODEL = 2048
D_MLSTM = 1024
N_HEADS = 4
HEAD_DIM = D_MLSTM // N_HEADS
D_POOL = D_MODEL - D_MLSTM
POOL_WINDOWS = (2, 4, 8, 16)
POOL_GROUP_DIM = D_POOL // len(POOL_WINDOWS)
D_FF = 5632
LN_EPS = 1e-5

LANES = 128
SUBLANES = 8
HALO = SUBLANES
CHUNK = 256
GATE_COLS = 2 * LANES
D_IN_PACKED = 4 * D_MLSTM + D_POOL + GATE_COLS
FF_SUB = 256
FF_TILE = 2 * FF_SUB
ROW_CHUNK = 128
VMEM_LIMIT = 56 * 1024 * 1024

MXU_DTYPE = jnp.bfloat16
F32 = jnp.float32


def _mm(a, b):
    return jnp.dot(a, b, preferred_element_type=F32)


def _layer_norm_rows(y, g, b):
    mu = jnp.mean(y, axis=-1, keepdims=True)
    d = y - mu
    var = jnp.mean(d * d, axis=-1, keepdims=True)
    return d * lax.rsqrt(var + LN_EPS) * g + b


def _const_spec(shape):
    nd = len(shape)
    return pl.BlockSpec(shape, lambda *_: (0,) * nd, pipeline_mode=pl.Buffered(1))


def _inproj_kernel(x_ref, w_ref, wu_ref, wg_ref, b_ref,
                   q_ref, k_ref, v_ref, o_ref, u_ref, gi_ref, gf_ref):
    xb = x_ref[...].astype(MXU_DTYPE)
    d = D_MLSTM

    def proj(c0, c1):
        return _mm(xb, w_ref[:, c0:c1]) + b_ref[:, c0:c1]

    q_ref[...] = proj(0, d).astype(q_ref.dtype)
    k_ref[...] = (proj(d, 2 * d) * (HEAD_DIM ** -0.5)).astype(k_ref.dtype)
    v_ref[...] = proj(2 * d, 3 * d).astype(v_ref.dtype)
    o_ref[...] = proj(3 * d, 4 * d)
    u_ref[...] = _mm(xb, wu_ref[...]) + b_ref[:, 4 * d:4 * d + D_POOL]
    g = _mm(xb, wg_ref[...]) + b_ref[:, 4 * d + D_POOL:]
    gi_ref[...] = g[:, :LANES]
    fp = g[:, LANES:]
    gf_ref[...] = jnp.minimum(fp, 0.0) - jnp.log1p(jnp.exp(-jnp.abs(fp)))


def _inproj(x2d, w_qkvo, w_u, w_gates, b_packed, tm):
    n = x2d.shape[0]
    row = lambda w: pl.BlockSpec((tm, w), lambda i: (i, 0))
    out_shape = (
        jax.ShapeDtypeStruct((n, D_MLSTM), MXU_DTYPE),
        jax.ShapeDtypeStruct((n, D_MLSTM), MXU_DTYPE),
        jax.ShapeDtypeStruct((n, D_MLSTM), MXU_DTYPE),
        jax.ShapeDtypeStruct((n, D_MLSTM), F32),
        jax.ShapeDtypeStruct((n, D_POOL), F32),
        jax.ShapeDtypeStruct((n, LANES), F32),
        jax.ShapeDtypeStruct((n, LANES), F32),
    )
    return pl.pallas_call(
        _inproj_kernel,
        grid=(n // tm,),
        in_specs=[row(D_MODEL), _const_spec((D_MODEL, 4 * D_MLSTM)), _const_spec((D_MODEL, D_POOL)),
                  _const_spec((D_MODEL, GATE_COLS)), _const_spec((1, D_IN_PACKED))],
        out_specs=(row(D_MLSTM), row(D_MLSTM), row(D_MLSTM), row(D_MLSTM), row(D_POOL),
                   row(LANES), row(LANES)),
        out_shape=out_shape,
        compiler_params=pltpu.CompilerParams(
            dimension_semantics=("parallel",), vmem_limit_bytes=VMEM_LIMIT),
        name="inproj",
    )(x2d, w_qkvo, w_u, w_gates, b_packed)


def _scan_rows(x, reverse, combine, identity):
    n = x.shape[0]
    ridx = lax.broadcasted_iota(jnp.int32, x.shape, 0)
    sh = 1
    while sh < n:
        if reverse:
            shifted = pltpu.roll(x, n - sh, axis=0)
            valid = ridx < n - sh
        else:
            shifted = pltpu.roll(x, sh, axis=0)
            valid = ridx >= sh
        x = combine(x, jnp.where(valid, shifted, identity))
        sh *= 2
    return x


def _mlstm_kernel(qf_ref, kf_ref, vf_ref, gif_ref, gff_ref,
                  qb_ref, kb_ref, vb_ref, gib_ref, gfb_ref,
                  hf_ref, hb_ref, c_ref, m_ref):
    L = CHUNK

    @pl.when(pl.program_id(1) == 0)
    def _():
        c_ref[...] = jnp.zeros_like(c_ref)
        m_ref[...] = jnp.zeros_like(m_ref)

    t_idx = lax.broadcasted_iota(jnp.int32, (L, L), 0)
    s_idx = lax.broadcasted_iota(jnp.int32, (L, L), 1)
    m_prev = m_ref[0:1, :]
    ones_ext = jnp.ones((L, LANES), MXU_DTYPE)
    m_new_rows = []

    for direction, (q_ref, k_ref, v_ref, gi_ref, gf_ref, h_ref) in enumerate((
            (qf_ref, kf_ref, vf_ref, gif_ref, gff_ref, hf_ref),
            (qb_ref, kb_ref, vb_ref, gib_ref, gfb_ref, hb_ref))):
        reverse = direction == 1
        mask = (s_idx >= t_idx) if reverse else (s_idx <= t_idx)
        last = 0 if reverse else L - 1
        b = _scan_rows(gf_ref[...], reverse, jnp.add, 0.0)
        a = gi_ref[...] - b
        big_m = jnp.maximum(_scan_rows(a, reverse, jnp.maximum, -jnp.inf), m_prev)
        w_inter = jnp.exp(m_prev - big_m)
        exp_neg_m = jnp.exp(-(b + big_m))
        m_last = big_m[last:last + 1, :]
        decay = jnp.exp(m_prev - m_last)
        w_state = jnp.exp(a - m_last)
        m_new_rows.append(b[last:last + 1, :] + m_last)
        a_t = a.T

        for h in range(N_HEADS):
            lane = direction * N_HEADS + h
            cols = slice(h * HEAD_DIM, (h + 1) * HEAD_DIM)
            q = q_ref[:, cols]
            k = k_ref[:, cols]
            v_ext = jnp.concatenate([v_ref[:, cols], ones_ext], axis=1)
            s_qk = lax.dot_general(q, k, (((1,), (1,)), ((), ())), preferred_element_type=F32)
            logw = jnp.where(mask, a_t[lane:lane + 1, :] - big_m[:, lane:lane + 1], -jnp.inf)
            s = (s_qk * jnp.exp(logw)).astype(MXU_DTYPE)
            c_old = c_ref[lane]
            r = _mm(s, v_ext) + w_inter[:, lane:lane + 1] * _mm(q, c_old.astype(MXU_DTYPE))
            den = jnp.maximum(jnp.abs(r[:, HEAD_DIM:]), exp_neg_m[:, lane:lane + 1])
            inv = 1.0 / den
            h_ref[:, cols] = r[:, :HEAD_DIM] * jnp.concatenate([inv] * (HEAD_DIM // LANES), axis=1)
            kw = (k.astype(F32) * w_state[:, lane:lane + 1]).astype(MXU_DTYPE)
            upd = lax.dot_general(kw, v_ext, (((0,), (0,)), ((), ())), preferred_element_type=F32)
            c_ref[lane] = decay[:, lane:lane + 1] * c_old + upd

    lane_idx = lax.broadcasted_iota(jnp.int32, (1, LANES), 1)
    m_ref[0:1, :] = jnp.where(lane_idx < N_HEADS, m_new_rows[0], m_new_rows[1])


def _mlstm(q, k, v, gi, gf, batch, seq):
    n = q.shape[0]
    nc = seq // CHUNK
    fwd = lambda w: pl.BlockSpec((CHUNK, w), lambda b, c: (b * nc + c, 0))
    bwd = lambda w: pl.BlockSpec((CHUNK, w), lambda b, c: (b * nc + nc - 1 - c, 0))
    return pl.pallas_call(
        _mlstm_kernel,
        grid=(batch, nc),
        in_specs=[fwd(D_MLSTM), fwd(D_MLSTM), fwd(D_MLSTM), fwd(LANES), fwd(LANES),
                  bwd(D_MLSTM), bwd(D_MLSTM), bwd(D_MLSTM), bwd(LANES), bwd(LANES)],
        out_specs=(fwd(D_MLSTM), bwd(D_MLSTM)),
        out_shape=(jax.ShapeDtypeStruct((n, D_MLSTM), F32), jax.ShapeDtypeStruct((n, D_MLSTM), F32)),
        scratch_shapes=[pltpu.VMEM((2 * N_HEADS, HEAD_DIM, HEAD_DIM + LANES), F32),
                        pltpu.VMEM((SUBLANES, LANES), F32)],
        compiler_params=pltpu.CompilerParams(
            dimension_semantics=("parallel", "arbitrary"), vmem_limit_bytes=VMEM_LIMIT),
        name="mlstm_scan",
    )(q, k, v, gi, gf, q, k, v, gi, gf)


def _outproj_kernel(hf_ref, hb_ref, o_ref, u_ref, up_ref, un_ref, x_ref,
                    mhg_ref, wp_ref, ps_ref, wo_ref, bo_ref, g_ref, b_ref,
                    y_ref, uext_ref, psum_ref, *, tiles_per_seq, seq, alpha):
    tm = x_ref.shape[0]
    t_in_seq = pl.program_id(0) % tiles_per_seq
    not_first = (t_in_seq > 0).astype(F32)
    not_last = (t_in_seq < tiles_per_seq - 1).astype(F32)

    uext_ref[0:HALO, :] = up_ref[...] * not_first
    uext_ref[HALO:HALO + tm, :] = u_ref[...]
    uext_ref[HALO + tm:, :] = un_ref[...] * not_last

    for r0 in range(0, tm, ROW_CHUNK):
        rows = slice(r0, r0 + ROW_CHUNK)
        hs = hf_ref[rows, :] + hb_ref[rows, :]
        parts = []
        for h in range(N_HEADS):
            xh = hs[:, h * HEAD_DIM:(h + 1) * HEAD_DIM]
            mu = jnp.mean(xh, axis=-1, keepdims=True)
            d = xh - mu
            var = jnp.mean(d * d, axis=-1, keepdims=True)
            parts.append(d * lax.rsqrt(var + LN_EPS))
        hm = jnp.concatenate(parts, axis=1) * mhg_ref[...] * jax.nn.sigmoid(o_ref[rows, :])

        pos = t_in_seq * tm + r0 + lax.broadcasted_iota(jnp.int32, (ROW_CHUNK, 1), 0)
        base = HALO + r0
        hp_parts = []
        for gi, w in enumerate(POOL_WINDOWS):
            cols = slice(gi * POOL_GROUP_DIM, (gi + 1) * POOL_GROUP_DIM)
            lo = base - w // 2
            n_rows = ROW_CHUNK + w - 2
            acc = uext_ref[lo:lo + n_rows, cols] + uext_ref[lo + 1:lo + 1 + n_rows, cols]
            span = 2
            while span < w:
                psum_ref[0:n_rows, cols] = acc
                n_rows -= span
                acc = psum_ref[0:n_rows, cols] + psum_ref[span:span + n_rows, cols]
                span *= 2
            cnt = (jnp.minimum(pos + (w - w // 2), seq) - jnp.maximum(pos - w // 2, 0)).astype(F32)
            pooled = acc / cnt - uext_ref[base:base + ROW_CHUNK, cols]
            hp_parts.append(_mm(pooled.astype(MXU_DTYPE), wp_ref[gi]))
        hp = jnp.concatenate(hp_parts, axis=1) * ps_ref[...]

        mixed = jnp.concatenate([hm, hp], axis=1).astype(MXU_DTYPE)
        mix = _mm(mixed, wo_ref[...]) + bo_ref[...]
        y_ref[rows, :] = _layer_norm_rows(alpha * x_ref[rows, :] + mix, g_ref[...], b_ref[...])


def _outproj(hf, hb, o, u, x2d, mh_g, w_pool, pool_scale, w_out, b_out, ln_g, ln_b, seq, tm, alpha):
    n = x2d.shape[0]
    tiles_per_seq = seq // tm
    hb_per_tile = tm // HALO
    n_halo_blocks = n // HALO
    row = lambda w: pl.BlockSpec((tm, w), lambda i: (i, 0))
    prev = pl.BlockSpec((HALO, D_POOL), lambda i: (jnp.maximum(i * hb_per_tile - 1, 0), 0))
    nxt = pl.BlockSpec((HALO, D_POOL),
                       lambda i: (jnp.minimum((i + 1) * hb_per_tile, n_halo_blocks - 1), 0))
    kern = functools.partial(_outproj_kernel, tiles_per_seq=tiles_per_seq, seq=seq, alpha=alpha)
    return pl.pallas_call(
        kern,
        grid=(n // tm,),
        in_specs=[row(D_MLSTM), row(D_MLSTM), row(D_MLSTM), row(D_POOL), prev, nxt, row(D_MODEL),
                  _const_spec((1, D_MLSTM)),
                  _const_spec((len(POOL_WINDOWS), POOL_GROUP_DIM, POOL_GROUP_DIM)),
                  _const_spec((1, D_POOL)), _const_spec((D_MODEL, D_MODEL)),
                  _const_spec((1, D_MODEL)), _const_spec((1, D_MODEL)), _const_spec((1, D_MODEL))],
        out_specs=row(D_MODEL),
        out_shape=jax.ShapeDtypeStruct((n, D_MODEL), F32),
        scratch_shapes=[pltpu.VMEM((tm + 2 * HALO, D_POOL), F32),
                        pltpu.VMEM((ROW_CHUNK + 2 * HALO, D_POOL), F32)],
        compiler_params=pltpu.CompilerParams(
            dimension_semantics=("parallel",), vmem_limit_bytes=VMEM_LIMIT),
        name="outproj_ln",
    )(hf, hb, o, u, u, u, x2d, mh_g, w_pool, pool_scale, w_out, b_out, ln_g, ln_b)


def _ffn_kernel(x_ref, xp_ref, xn_ref, wg_ref, wv_ref, vg_ref, vv_ref, wd_ref, bd_ref, g_ref, b_ref,
                y_ref, xs_ref, a0_ref, a1_ref, *, tiles_per_seq, alpha):
    tm = x_ref.shape[0]
    j = pl.program_id(1)
    t_in_seq = pl.program_id(0) % tiles_per_seq
    not_first = (t_in_seq > 0).astype(F32)
    not_last = (t_in_seq < tiles_per_seq - 1).astype(F32)

    @pl.when(j == 0)
    def _():
        x = x_ref[...]
        xs_ref[...] = jnp.concatenate([xp_ref[...], x, xn_ref[...]], axis=0).astype(xs_ref.dtype)
        y_ref[...] = alpha * x + bd_ref[...]

    xs = xs_ref[...]
    fs = []
    for sub, a_ref in enumerate((a0_ref, a1_ref)):
        src = slice(sub * FF_SUB, (sub + 1) * FF_SUB)
        for part, (w_ref, vec_ref) in enumerate(((wg_ref, vg_ref), (wv_ref, vv_ref))):
            dst = slice(part * FF_SUB, (part + 1) * FF_SUB)
            a = _mm(xs, w_ref[:, src]) + vec_ref[0:1, src]
            a_ref[0:HALO, dst] = a[0:HALO] * not_first
            a_ref[HALO:HALO + tm, dst] = a[HALO:HALO + tm]
            a_ref[HALO + tm:, dst] = a[HALO + tm:] * not_last
        conv = []
        for part, vec_ref in enumerate((vg_ref, vv_ref)):
            dst = slice(part * FF_SUB, (part + 1) * FF_SUB)
            conv.append(vec_ref[1:2, src] * a_ref[HALO - 1:HALO - 1 + tm, dst]
                        + vec_ref[2:3, src] * a_ref[HALO:HALO + tm, dst]
                        + vec_ref[3:4, src] * a_ref[HALO + 1:HALO + 1 + tm, dst] + vec_ref[4:5, src])
        gate, val = conv
        fs.append((gate * jax.nn.sigmoid(gate) * val).astype(MXU_DTYPE))
    for sub in range(len(fs)):
        for cols in (slice(0, D_MODEL // 2), slice(D_MODEL // 2, D_MODEL)):
            y_ref[:, cols] += _mm(fs[sub], wd_ref[sub * FF_SUB:(sub + 1) * FF_SUB, cols])

    @pl.when(j == pl.num_programs(1) - 1)
    def _():
        y_ref[...] = _layer_norm_rows(y_ref[...], g_ref[...], b_ref[...])


def _ffn(x1, w_up, ff_vecs, w_down, b_down, ln_g, ln_b, seq, tm, tf, alpha):
    n = x1.shape[0]
    tiles_per_seq = seq // tm
    hb_per_tile = tm // HALO
    n_halo_blocks = n // HALO
    nj = D_FF // tf
    main = pl.BlockSpec((tm, D_MODEL), lambda i, j: (i, 0))
    prev = pl.BlockSpec((HALO, D_MODEL), lambda i, j: (jnp.maximum(i * hb_per_tile - 1, 0), 0))
    nxt = pl.BlockSpec((HALO, D_MODEL),
                       lambda i, j: (jnp.minimum((i + 1) * hb_per_tile, n_halo_blocks - 1), 0))
    up_gate = pl.BlockSpec((D_MODEL, tf), lambda i, j: (0, j))
    up_val = pl.BlockSpec((D_MODEL, tf), lambda i, j: (0, j + nj))
    vec_gate = pl.BlockSpec((SUBLANES, tf), lambda i, j: (0, j))
    vec_val = pl.BlockSpec((SUBLANES, tf), lambda i, j: (0, j + nj))
    down = pl.BlockSpec((tf, D_MODEL), lambda i, j: (j, 0))
    vec = pl.BlockSpec((1, D_MODEL), lambda i, j: (0, 0))
    kern = functools.partial(_ffn_kernel, tiles_per_seq=tiles_per_seq, alpha=alpha)
    return pl.pallas_call(
        kern,
        grid=(n // tm, nj),
        in_specs=[main, prev, nxt, up_gate, up_val, vec_gate, vec_val, down, vec, vec, vec],
        out_specs=main,
        out_shape=jax.ShapeDtypeStruct((n, D_MODEL), F32),
        scratch_shapes=[pltpu.VMEM((tm + 2 * HALO, D_MODEL), MXU_DTYPE),
                        pltpu.VMEM((tm + 2 * HALO, 2 * FF_SUB), F32),
                        pltpu.VMEM((tm + 2 * HALO, 2 * FF_SUB), F32)],
        compiler_params=pltpu.CompilerParams(
            dimension_semantics=("parallel", "arbitrary"), vmem_limit_bytes=VMEM_LIMIT),
        name="ffn_ln",
    )(x1, x1, x1, w_up, w_up, ff_vecs, ff_vecs, w_down, b_down, ln_g, ln_b)


def _pack_layer(w_in, b_in, mh_norm_g, w_pool, pool_scale, w_out, b_out, ln1_g, ln1_b,
                w_up, b_up, w_conv, b_conv, w_down, b_down, ln2_g, ln2_b):
    g0 = 4 * D_MLSTM
    h = N_HEADS
    ng = 4 * h

    def gate_cols(a):
        gts = a[..., g0:g0 + ng]
        pad = jnp.zeros(a.shape[:-1] + (LANES - 2 * h,), a.dtype)
        return jnp.concatenate([gts[..., 0:h], gts[..., 2 * h:3 * h], pad,
                                gts[..., h:2 * h], gts[..., 3 * h:4 * h], pad], axis=-1)

    pack_in = lambda a: jnp.concatenate([a[..., :g0], a[..., g0 + ng:], gate_cols(a)], axis=-1)
    row = lambda a: a.reshape(1, -1).astype(F32)

    return dict(
        w_qkvo=w_in[:, :g0].astype(MXU_DTYPE), w_u=w_in[:, g0 + ng:].astype(MXU_DTYPE),
        w_gates=gate_cols(w_in).astype(MXU_DTYPE), b_in=row(pack_in(b_in)),
        mh_g=row(mh_norm_g), w_pool=w_pool.astype(MXU_DTYPE), pool_scale=row(pool_scale),
        w_out=w_out.astype(MXU_DTYPE), b_out=row(b_out), ln1_g=row(ln1_g), ln1_b=row(ln1_b),
        w_up=w_up.astype(MXU_DTYPE),
        ff_vecs=jnp.concatenate(
            [b_up[None], w_conv, b_conv[None], jnp.zeros((SUBLANES - 5, 2 * D_FF), F32)],
            axis=0).astype(F32),
        w_down=w_down.astype(MXU_DTYPE), b_down=row(b_down), ln2_g=row(ln2_g), ln2_b=row(ln2_b),
    )


def _tile(seq, want):
    return min(want, seq)


def _layer(x2d, p, batch, seq, alpha):
    q, k, v, o, u, gi, gf = _inproj(x2d, p["w_qkvo"], p["w_u"], p["w_gates"], p["b_in"],
                                    _tile(seq, 512))
    hf, hb = _mlstm(q, k, v, gi, gf, batch, seq)
    x1 = _outproj(hf, hb, o, u, x2d, p["mh_g"], p["w_pool"], p["pool_scale"], p["w_out"], p["b_out"],
                  p["ln1_g"], p["ln1_b"], seq, _tile(seq, 512), alpha)
    return _ffn(x1, p["w_up"], p["ff_vecs"], p["w_down"], p["b_down"], p["ln2_g"], p["ln2_b"],
                seq, _tile(seq, 512), FF_TILE, alpha)


def kernel(x_prompt, x_sample, w_in, b_in, mh_norm_g, w_pool, pool_scale, w_out, b_out, ln1_g, ln1_b,
           w_up, b_up, w_conv, b_conv, w_down, b_down, ln2_g, ln2_b):
    weights = (w_in, b_in, mh_norm_g, w_pool, pool_scale, w_out, b_out, ln1_g, ln1_b,
               w_up, b_up, w_conv, b_conv, w_down, b_down, ln2_g, ln2_b)
    depth = w_in.shape[0]
    alpha = (2.0 * depth) ** 0.25
    layers = [_pack_layer(*(w[l] for w in weights)) for l in range(depth)]

    def trunk(x):
        batch, seq, _ = x.shape
        x2d = x.reshape(batch * seq, D_MODEL)
        for p in layers:
            x2d = _layer(x2d, p, batch, seq, alpha)
        return x2d.reshape(batch, seq, D_MODEL)

    return trunk(x_prompt), trunk(x_sample)
```

```python
import functools

import jax
import jax.numpy as jnp
from jax import lax
from jax.experimental import pallas as pl
from jax.experimental.pallas import tpu as pltpu

D_MODEL = 2048
D_MLSTM = 1024
N_HEADS = 4
HEAD_DIM = D_MLSTM // N_HEADS
D_POOL = D_MODEL - D_MLSTM
POOL_WINDOWS = (2, 4, 8, 16)
POOL_GROUP_DIM = D_POOL // len(POOL_WINDOWS)
D_FF = 5632
LN_EPS = 1e-5

LANES = 128
SUBLANES = 8
HALO = SUBLANES
CHUNK = 256
GATE_COLS = 2 * LANES
D_IN_PACKED = 4 * D_MLSTM + D_POOL + GATE_COLS
FF_TILE = 512
ROW_CHUNK = 128
VMEM_LIMIT = 56 * 1024 * 1024

MXU_DTYPE = jnp.bfloat16
F32 = jnp.float32


def _mm(a, b):
    return jnp.dot(a, b, preferred_element_type=F32)


def _layer_norm_rows(y, g, b):
    mu = jnp.mean(y, axis=-1, keepdims=True)
    d = y - mu
    var = jnp.mean(d * d, axis=-1, keepdims=True)
    return d * lax.rsqrt(var + LN_EPS) * g + b


def _const_spec(shape):
    nd = len(shape)
    return pl.BlockSpec(shape, lambda *_: (0,) * nd, pipeline_mode=pl.Buffered(1))


def _inproj_kernel(x_ref, w_ref, wu_ref, wg_ref, b_ref,
                   q_ref, k_ref, v_ref, o_ref, u_ref, gi_ref, gf_ref):
    xb = x_ref[...].astype(MXU_DTYPE)
    d = D_MLSTM

    def proj(c0, c1):
        return _mm(xb, w_ref[:, c0:c1]) + b_ref[:, c0:c1]

    q_ref[...] = proj(0, d).astype(q_ref.dtype)
    k_ref[...] = (proj(d, 2 * d) * (HEAD_DIM ** -0.5)).astype(k_ref.dtype)
    v_ref[...] = proj(2 * d, 3 * d).astype(v_ref.dtype)
    o_ref[...] = proj(3 * d, 4 * d)
    u_ref[...] = _mm(xb, wu_ref[...]) + b_ref[:, 4 * d:4 * d + D_POOL]
    g = _mm(xb, wg_ref[...]) + b_ref[:, 4 * d + D_POOL:]
    gi_ref[...] = g[:, :LANES]
    fp = g[:, LANES:]
    gf_ref[...] = jnp.minimum(fp, 0.0) - jnp.log1p(jnp.exp(-jnp.abs(fp)))


def _inproj(x2d, w_qkvo, w_u, w_gates, b_packed, tm):
    n = x2d.shape[0]
    row = lambda w: pl.BlockSpec((tm, w), lambda i: (i, 0))
    out_shape = (
        jax.ShapeDtypeStruct((n, D_MLSTM), MXU_DTYPE),
        jax.ShapeDtypeStruct((n, D_MLSTM), MXU_DTYPE),
        jax.ShapeDtypeStruct((n, D_MLSTM), MXU_DTYPE),
        jax.ShapeDtypeStruct((n, D_MLSTM), F32),
        jax.ShapeDtypeStruct((n, D_POOL), F32),
        jax.ShapeDtypeStruct((n, LANES), F32),
        jax.ShapeDtypeStruct((n, LANES), F32),
    )
    return pl.pallas_call(
        _inproj_kernel,
        grid=(n // tm,),
        in_specs=[row(D_MODEL), _const_spec((D_MODEL, 4 * D_MLSTM)), _const_spec((D_MODEL, D_POOL)),
                  _const_spec((D_MODEL, GATE_COLS)), _const_spec((1, D_IN_PACKED))],
        out_specs=(row(D_MLSTM), row(D_MLSTM), row(D_MLSTM), row(D_MLSTM), row(D_POOL),
                   row(LANES), row(LANES)),
        out_shape=out_shape,
        compiler_params=pltpu.CompilerParams(
            dimension_semantics=("parallel",), vmem_limit_bytes=VMEM_LIMIT),
        name="inproj",
    )(x2d, w_qkvo, w_u, w_gates, b_packed)


def _scan_rows(x, reverse, combine, identity):
    n = x.shape[0]
    ridx = lax.broadcasted_iota(jnp.int32, x.shape, 0)
    sh = 1
    while sh < n:
        if reverse:
            shifted = pltpu.roll(x, n - sh, axis=0)
            valid = ridx < n - sh
        else:
            shifted = pltpu.roll(x, sh, axis=0)
            valid = ridx >= sh
        x = combine(x, jnp.where(valid, shifted, identity))
        sh *= 2
    return x


def _mlstm_kernel(qf_ref, kf_ref, vf_ref, gif_ref, gff_ref,
                  qb_ref, kb_ref, vb_ref, gib_ref, gfb_ref,
                  hf_ref, hb_ref, c_ref, m_ref):
    L = CHUNK

    @pl.when(pl.program_id(1) == 0)
    def _():
        c_ref[...] = jnp.zeros_like(c_ref)
        m_ref[...] = jnp.zeros_like(m_ref)

    t_idx = lax.broadcasted_iota(jnp.int32, (L, L), 0)
    s_idx = lax.broadcasted_iota(jnp.int32, (L, L), 1)
    m_prev = m_ref[0:1, :]
    ones_ext = jnp.ones((L, LANES), MXU_DTYPE)
    m_new_rows = []

    for direction, (q_ref, k_ref, v_ref, gi_ref, gf_ref, h_ref) in enumerate((
            (qf_ref, kf_ref, vf_ref, gif_ref, gff_ref, hf_ref),
            (qb_ref, kb_ref, vb_ref, gib_ref, gfb_ref, hb_ref))):
        reverse = direction == 1
        mask = (s_idx >= t_idx) if reverse else (s_idx <= t_idx)
        last = 0 if reverse else L - 1
        b = _scan_rows(gf_ref[...], reverse, jnp.add, 0.0)
        a = gi_ref[...] - b
        big_m = jnp.maximum(_scan_rows(a, reverse, jnp.maximum, -jnp.inf), m_prev)
        w_inter = jnp.exp(m_prev - big_m)
        exp_neg_m = jnp.exp(-(b + big_m))
        m_last = big_m[last:last + 1, :]
        decay = jnp.exp(m_prev - m_last)
        w_state = jnp.exp(a - m_last)
        m_new_rows.append(b[last:last + 1, :] + m_last)
        a_t = a.T

        for h in range(N_HEADS):
            lane = direction * N_HEADS + h
            cols = slice(h * HEAD_DIM, (h + 1) * HEAD_DIM)
            q = q_ref[:, cols]
            k = k_ref[:, cols]
            v_ext = jnp.concatenate([v_ref[:, cols], ones_ext], axis=1)
            s_qk = lax.dot_general(q, k, (((1,), (1,)), ((), ())), preferred_element_type=F32)
            logw = jnp.where(mask, a_t[lane:lane + 1, :] - big_m[:, lane:lane + 1], -jnp.inf)
            s = (s_qk * jnp.exp(logw)).astype(MXU_DTYPE)
            c_old = c_ref[lane]
            r = _mm(s, v_ext) + w_inter[:, lane:lane + 1] * _mm(q, c_old.astype(MXU_DTYPE))
            den = jnp.maximum(jnp.abs(r[:, HEAD_DIM:]), exp_neg_m[:, lane:lane + 1])
            inv = 1.0 / den
            h_ref[:, cols] = r[:, :HEAD_DIM] * jnp.concatenate([inv] * (HEAD_DIM // LANES), axis=1)
            kw = (k.astype(F32) * w_state[:, lane:lane + 1]).astype(MXU_DTYPE)
            upd = lax.dot_general(kw, v_ext, (((0,), (0,)), ((), ())), preferred_element_type=F32)
            c_ref[lane] = decay[:, lane:lane + 1] * c_old + upd

    lane_idx = lax.broadcasted_iota(jnp.int32, (1, LANES), 1)
    m_ref[0:1, :] = jnp.where(lane_idx < N_HEADS, m_new_rows[0], m_new_rows[1])


def _mlstm(q, k, v, gi, gf, batch, seq):
    n = q.shape[0]
    nc = seq // CHUNK
    fwd = lambda w: pl.BlockSpec((CHUNK, w), lambda b, c: (b * nc + c, 0))
    bwd = lambda w: pl.BlockSpec((CHUNK, w), lambda b, c: (b * nc + nc - 1 - c, 0))
    return pl.pallas_call(
        _mlstm_kernel,
        grid=(batch, nc),
        in_specs=[fwd(D_MLSTM), fwd(D_MLSTM), fwd(D_MLSTM), fwd(LANES), fwd(LANES),
                  bwd(D_MLSTM), bwd(D_MLSTM), bwd(D_MLSTM), bwd(LANES), bwd(LANES)],
        out_specs=(fwd(D_MLSTM), bwd(D_MLSTM)),
        out_shape=(jax.ShapeDtypeStruct((n, D_MLSTM), F32), jax.ShapeDtypeStruct((n, D_MLSTM), F32)),
        scratch_shapes=[pltpu.VMEM((2 * N_HEADS, HEAD_DIM, HEAD_DIM + LANES), F32),
                        pltpu.VMEM((SUBLANES, LANES), F32)],
        compiler_params=pltpu.CompilerParams(
            dimension_semantics=("parallel", "arbitrary"), vmem_limit_bytes=VMEM_LIMIT),
        name="mlstm_scan",
    )(q, k, v, gi, gf, q, k, v, gi, gf)


def _outproj_kernel(hf_ref, hb_ref, o_ref, u_ref, up_ref, un_ref, x_ref,
                    mhg_ref, wp_ref, ps_ref, wo_ref, bo_ref, g_ref, b_ref,
                    y_ref, uext_ref, psum_ref, *, tiles_per_seq, seq, alpha):
    tm = x_ref.shape[0]
    t_in_seq = pl.program_id(0) % tiles_per_seq
    not_first = (t_in_seq > 0).astype(F32)
    not_last = (t_in_seq < tiles_per_seq - 1).astype(F32)

    uext_ref[0:HALO, :] = up_ref[...] * not_first
    uext_ref[HALO:HALO + tm, :] = u_ref[...]
    uext_ref[HALO + tm:, :] = un_ref[...] * not_last

    for r0 in range(0, tm, ROW_CHUNK):
        rows = slice(r0, r0 + ROW_CHUNK)
        hs = hf_ref[rows, :] + hb_ref[rows, :]
        parts = []
        for h in range(N_HEADS):
            xh = hs[:, h * HEAD_DIM:(h + 1) * HEAD_DIM]
            mu = jnp.mean(xh, axis=-1, keepdims=True)
            d = xh - mu
            var = jnp.mean(d * d, axis=-1, keepdims=True)
            parts.append(d * lax.rsqrt(var + LN_EPS))
        hm = jnp.concatenate(parts, axis=1) * mhg_ref[...] * jax.nn.sigmoid(o_ref[rows, :])

        pos = t_in_seq * tm + r0 + lax.broadcasted_iota(jnp.int32, (ROW_CHUNK, 1), 0)
        base = HALO + r0
        hp_parts = []
        for gi, w in enumerate(POOL_WINDOWS):
            cols = slice(gi * POOL_GROUP_DIM, (gi + 1) * POOL_GROUP_DIM)
            lo = base - w // 2
            n_rows = ROW_CHUNK + w - 2
            acc = uext_ref[lo:lo + n_rows, cols] + uext_ref[lo + 1:lo + 1 + n_rows, cols]
            span = 2
            while span < w:
                psum_ref[0:n_rows, cols] = acc
                n_rows -= span
                acc = psum_ref[0:n_rows, cols] + psum_ref[span:span + n_rows, cols]
                span *= 2
            cnt = (jnp.minimum(pos + (w - w // 2), seq) - jnp.maximum(pos - w // 2, 0)).astype(F32)
            pooled = acc / cnt - uext_ref[base:base + ROW_CHUNK, cols]
            hp_parts.append(_mm(pooled.astype(MXU_DTYPE), wp_ref[gi]))
        hp = jnp.concatenate(hp_parts, axis=1) * ps_ref[...]

        mixed = jnp.concatenate([hm, hp], axis=1).astype(MXU_DTYPE)
        mix = _mm(mixed, wo_ref[...]) + bo_ref[...]
        y_ref[rows, :] = _layer_norm_rows(alpha * x_ref[rows, :] + mix, g_ref[...], b_ref[...])


def _outproj(hf, hb, o, u, x2d, mh_g, w_pool, pool_scale, w_out, b_out, ln_g, ln_b, seq, tm, alpha):
    n = x2d.shape[0]
    tiles_per_seq = seq // tm
    hb_per_tile = tm // HALO
    n_halo_blocks = n // HALO
    row = lambda w: pl.BlockSpec((tm, w), lambda i: (i, 0))
    prev = pl.BlockSpec((HALO, D_POOL), lambda i: (jnp.maximum(i * hb_per_tile - 1, 0), 0))
    nxt = pl.BlockSpec((HALO, D_POOL),
                       lambda i: (jnp.minimum((i + 1) * hb_per_tile, n_halo_blocks - 1), 0))
    kern = functools.partial(_outproj_kernel, tiles_per_seq=tiles_per_seq, seq=seq, alpha=alpha)
    return pl.pallas_call(
        kern,
        grid=(n // tm,),
        in_specs=[row(D_MLSTM), row(D_MLSTM), row(D_MLSTM), row(D_POOL), prev, nxt, row(D_MODEL),
                  _const_spec((1, D_MLSTM)),
                  _const_spec((len(POOL_WINDOWS), POOL_GROUP_DIM, POOL_GROUP_DIM)),
                  _const_spec((1, D_POOL)), _const_spec((D_MODEL, D_MODEL)),
                  _const_spec((1, D_MODEL)), _const_spec((1, D_MODEL)), _const_spec((1, D_MODEL))],
        out_specs=row(D_MODEL),
        out_shape=jax.ShapeDtypeStruct((n, D_MODEL), F32),
        scratch_shapes=[pltpu.VMEM((tm + 2 * HALO, D_POOL), F32),
                        pltpu.VMEM((ROW_CHUNK + 2 * HALO, D_POOL), F32)],
        compiler_params=pltpu.CompilerParams(
            dimension_semantics=("parallel",), vmem_limit_bytes=VMEM_LIMIT),
        name="outproj_ln",
    )(hf, hb, o, u, u, u, x2d, mh_g, w_pool, pool_scale, w_out, b_out, ln_g, ln_b)


def _ffn_kernel(x_ref, xp_ref, xn_ref, wg_ref, wv_ref, vg_ref, vv_ref, wd_ref, bd_ref, g_ref, b_ref,
                y_ref, xs_ref, a_ref, *, tiles_per_seq, alpha):
    tm = x_ref.shape[0]
    tf = wd_ref.shape[0]
    j = pl.program_id(1)
    t_in_seq = pl.program_id(0) % tiles_per_seq
    not_first = (t_in_seq > 0).astype(F32)
    not_last = (t_in_seq < tiles_per_seq - 1).astype(F32)

    @pl.when(j == 0)
    def _():
        x = x_ref[...]
        xs_ref[...] = jnp.concatenate([xp_ref[...], x, xn_ref[...]], axis=0).astype(xs_ref.dtype)
        y_ref[...] = alpha * x + bd_ref[...]

    xs = xs_ref[...]
    conv = []
    for part, (w_ref, vec_ref) in enumerate(((wg_ref, vg_ref), (wv_ref, vv_ref))):
        cols = slice(part * tf, (part + 1) * tf)
        a = _mm(xs, w_ref[...]) + vec_ref[0:1, :]
        a_ref[0:HALO, cols] = a[0:HALO] * not_first
        a_ref[HALO:HALO + tm, cols] = a[HALO:HALO + tm]
        a_ref[HALO + tm:, cols] = a[HALO + tm:] * not_last
        conv.append(vec_ref[1:2, :] * a_ref[HALO - 1:HALO - 1 + tm, cols]
                    + vec_ref[2:3, :] * a_ref[HALO:HALO + tm, cols]
                    + vec_ref[3:4, :] * a_ref[HALO + 1:HALO + 1 + tm, cols] + vec_ref[4:5, :])
    gate, val = conv
    f = (gate * jax.nn.sigmoid(gate) * val).astype(MXU_DTYPE)
    for cols in (slice(0, D_MODEL // 2), slice(D_MODEL // 2, D_MODEL)):
        y_ref[:, cols] += _mm(f, wd_ref[:, cols])

    @pl.when(j == pl.num_programs(1) - 1)
    def _():
        y_ref[...] = _layer_norm_rows(y_ref[...], g_ref[...], b_ref[...])


def _ffn(x1, w_up, ff_vecs, w_down, b_down, ln_g, ln_b, seq, tm, tf, alpha):
    n = x1.shape[0]
    tiles_per_seq = seq // tm
    hb_per_tile = tm // HALO
    n_halo_blocks = n // HALO
    nj = D_FF // tf
    main = pl.BlockSpec((tm, D_MODEL), lambda i, j: (i, 0))
    prev = pl.BlockSpec((HALO, D_MODEL), lambda i, j: (jnp.maximum(i * hb_per_tile - 1, 0), 0))
    nxt = pl.BlockSpec((HALO, D_MODEL),
                       lambda i, j: (jnp.minimum((i + 1) * hb_per_tile, n_halo_blocks - 1), 0))
    up_gate = pl.BlockSpec((D_MODEL, tf), lambda i, j: (0, j))
    up_val = pl.BlockSpec((D_MODEL, tf), lambda i, j: (0, j + nj))
    vec_gate = pl.BlockSpec((SUBLANES, tf), lambda i, j: (0, j))
    vec_val = pl.BlockSpec((SUBLANES, tf), lambda i, j: (0, j + nj))
    down = pl.BlockSpec((tf, D_MODEL), lambda i, j: (j, 0))
    vec = pl.BlockSpec((1, D_MODEL), lambda i, j: (0, 0))
    kern = functools.partial(_ffn_kernel, tiles_per_seq=tiles_per_seq, alpha=alpha)
    return pl.pallas_call(
        kern,
        grid=(n // tm, nj),
        in_specs=[main, prev, nxt, up_gate, up_val, vec_gate, vec_val, down, vec, vec, vec],
        out_specs=main,
        out_shape=jax.ShapeDtypeStruct((n, D_MODEL), F32),
        scratch_shapes=[pltpu.VMEM((tm + 2 * HALO, D_MODEL), MXU_DTYPE),
                        pltpu.VMEM((tm + 2 * HALO, 2 * tf), F32)],
        compiler_params=pltpu.CompilerParams(
            dimension_semantics=("parallel", "arbitrary"), vmem_limit_bytes=VMEM_LIMIT),
        name="ffn_ln",
    )(x1, x1, x1, w_up, w_up, ff_vecs, ff_vecs, w_down, b_down, ln_g, ln_b)


def _pack_layer(w_in, b_in, mh_norm_g, w_pool, pool_scale, w_out, b_out, ln1_g, ln1_b,
                w_up, b_up, w_conv, b_conv, w_down, b_down, ln2_g, ln2_b):
    g0 = 4 * D_MLSTM
    h = N_HEADS
    ng = 4 * h

    def gate_cols(a):
        gts = a[..., g0:g0 + ng]
        pad = jnp.zeros(a.shape[:-1] + (LANES - 2 * h,), a.dtype)
        return jnp.concatenate([gts[..., 0:h], gts[..., 2 * h:3 * h], pad,
                                gts[..., h:2 * h], gts[..., 3 * h:4 * h], pad], axis=-1)

    pack_in = lambda a: jnp.concatenate([a[..., :g0], a[..., g0 + ng:], gate_cols(a)], axis=-1)
    row = lambda a: a.reshape(1, -1).astype(F32)

    return dict(
        w_qkvo=w_in[:, :g0].astype(MXU_DTYPE), w_u=w_in[:, g0 + ng:].astype(MXU_DTYPE),
        w_gates=gate_cols(w_in).astype(MXU_DTYPE), b_in=row(pack_in(b_in)),
        mh_g=row(mh_norm_g), w_pool=w_pool.astype(MXU_DTYPE), pool_scale=row(pool_scale),
        w_out=w_out.astype(MXU_DTYPE), b_out=row(b_out), ln1_g=row(ln1_g), ln1_b=row(ln1_b),
        w_up=w_up.astype(MXU_DTYPE),
        ff_vecs=jnp.concatenate(
            [b_up[None], w_conv, b_conv[None], jnp.zeros((SUBLANES - 5, 2 * D_FF), F32)],
            axis=0).astype(F32),
        w_down=w_down.astype(MXU_DTYPE), b_down=row(b_down), ln2_g=row(ln2_g), ln2_b=row(ln2_b),
    )


def _tile(seq, want):
    return min(want, seq)


def _layer(x2d, p, batch, seq, alpha):
    q, k, v, o, u, gi, gf = _inproj(x2d, p["w_qkvo"], p["w_u"], p["w_gates"], p["b_in"],
                                    _tile(seq, 512))
    hf, hb = _mlstm(q, k, v, gi, gf, batch, seq)
    x1 = _outproj(hf, hb, o, u, x2d, p["mh_g"], p["w_pool"], p["pool_scale"], p["w_out"], p["b_out"],
                  p["ln1_g"], p["ln1_b"], seq, _tile(seq, 512), alpha)
    return _ffn(x1, p["w_up"], p["ff_vecs"], p["w_down"], p["b_down"], p["ln2_g"], p["ln2_b"],
                seq, _tile(seq, 512), FF_TILE, alpha)


def kernel(x_prompt, x_sample, w_in, b_in, mh_norm_g, w_pool, pool_scale, w_out, b_out, ln1_g, ln1_b,
           w_up, b_up, w_conv, b_conv, w_down, b_down, ln2_g, ln2_b):
    weights = (w_in, b_in, mh_norm_g, w_pool, pool_scale, w_out, b_out, ln1_g, ln1_b,
               w_up, b_up, w_conv, b_conv, w_down, b_down, ln2_g, ln2_b)
    depth = w_in.shape[0]
    alpha = (2.0 * depth) ** 0.25
    layers = [_pack_layer(*(w[l] for w in weights)) for l in range(depth)]

    def trunk(x):
        batch, seq, _ = x.shape
        x2d = x.reshape(batch * seq, D_MODEL)
        for p in layers:
            x2d = _layer(x2d, p, batch, seq, alpha)
        return x2d.reshape(batch, seq, D_MODEL)

    return trunk(x_prompt), trunk(x_sample)
```

```python
import functools

import jax
import jax.numpy as jnp
from jax import lax
from jax.experimental import pallas as pl
from jax.experimental.pallas import tpu as pltpu

D_MODEL = 2048
D_MLSTM = 1024
N_HEADS = 4
HEAD_DIM = D_MLSTM // N_HEADS
D_POOL = D_MODEL - D_MLSTM
POOL_WINDOWS = (2, 4, 8, 16)
POOL_GROUP_DIM = D_POOL // len(POOL_WINDOWS)
D_FF = 5632
LN_EPS = 1e-5

LANES = 128
SUBLANES = 8
HALO = SUBLANES
CHUNK = 256
GATE_COLS = 2 * LANES
D_IN_PACKED = 4 * D_MLSTM + D_POOL + GATE_COLS
FF_TILE = 512
ROW_CHUNK = 128
VMEM_LIMIT = 56 * 1024 * 1024

MXU_DTYPE = jnp.bfloat16
F32 = jnp.float32


def _mm(a, b):
    return jnp.dot(a, b, preferred_element_type=F32)


def _layer_norm_rows(y, g, b):
    mu = jnp.mean(y, axis=-1, keepdims=True)
    d = y - mu
    var = jnp.mean(d * d, axis=-1, keepdims=True)
    return d * lax.rsqrt(var + LN_EPS) * g + b


def _const_spec(shape):
    nd = len(shape)
    return pl.BlockSpec(shape, lambda *_: (0,) * nd, pipeline_mode=pl.Buffered(1))


def _inproj_kernel(x_ref, w_ref, wu_ref, wg_ref, b_ref,
                   q_ref, k_ref, v_ref, o_ref, u_ref, gi_ref, gf_ref):
    xb = x_ref[...].astype(MXU_DTYPE)
    d = D_MLSTM

    def proj(c0, c1):
        return _mm(xb, w_ref[:, c0:c1]) + b_ref[:, c0:c1]

    q_ref[...] = proj(0, d).astype(q_ref.dtype)
    k_ref[...] = (proj(d, 2 * d) * (HEAD_DIM ** -0.5)).astype(k_ref.dtype)
    v_ref[...] = proj(2 * d, 3 * d).astype(v_ref.dtype)
    o_ref[...] = proj(3 * d, 4 * d)
    u_ref[...] = _mm(xb, wu_ref[...]) + b_ref[:, 4 * d:4 * d + D_POOL]
    g = _mm(xb, wg_ref[...]) + b_ref[:, 4 * d + D_POOL:]
    gi_ref[...] = g[:, :LANES]
    fp = g[:, LANES:]
    gf_ref[...] = jnp.minimum(fp, 0.0) - jnp.log1p(jnp.exp(-jnp.abs(fp)))


def _inproj(x2d, w_qkvo, w_u, w_gates, b_packed, tm):
    n = x2d.shape[0]
    row = lambda w: pl.BlockSpec((tm, w), lambda i: (i, 0))
    out_shape = (
        jax.ShapeDtypeStruct((n, D_MLSTM), MXU_DTYPE),
        jax.ShapeDtypeStruct((n, D_MLSTM), MXU_DTYPE),
        jax.ShapeDtypeStruct((n, D_MLSTM), MXU_DTYPE),
        jax.ShapeDtypeStruct((n, D_MLSTM), F32),
        jax.ShapeDtypeStruct((n, D_POOL), F32),
        jax.ShapeDtypeStruct((n, LANES), F32),
        jax.ShapeDtypeStruct((n, LANES), F32),
    )
    return pl.pallas_call(
        _inproj_kernel,
        grid=(n // tm,),
        in_specs=[row(D_MODEL), _const_spec((D_MODEL, 4 * D_MLSTM)), _const_spec((D_MODEL, D_POOL)),
                  _const_spec((D_MODEL, GATE_COLS)), _const_spec((1, D_IN_PACKED))],
        out_specs=(row(D_MLSTM), row(D_MLSTM), row(D_MLSTM), row(D_MLSTM), row(D_POOL),
                   row(LANES), row(LANES)),
        out_shape=out_shape,
        compiler_params=pltpu.CompilerParams(
            dimension_semantics=("parallel",), vmem_limit_bytes=VMEM_LIMIT),
        name="inproj",
    )(x2d, w_qkvo, w_u, w_gates, b_packed)


def _scan_rows(x, reverse, combine, identity):
    n = x.shape[0]
    ridx = lax.broadcasted_iota(jnp.int32, x.shape, 0)
    sh = 1
    while sh < n:
        if reverse:
            shifted = pltpu.roll(x, n - sh, axis=0)
            valid = ridx < n - sh
        else:
            shifted = pltpu.roll(x, sh, axis=0)
            valid = ridx >= sh
        x = combine(x, jnp.where(valid, shifted, identity))
        sh *= 2
    return x


def _mlstm_kernel(qf_ref, kf_ref, vf_ref, gif_ref, gff_ref,
                  qb_ref, kb_ref, vb_ref, gib_ref, gfb_ref,
                  hf_ref, hb_ref, c_ref, m_ref):
    L = CHUNK

    @pl.when(pl.program_id(1) == 0)
    def _():
        c_ref[...] = jnp.zeros_like(c_ref)
        m_ref[...] = jnp.zeros_like(m_ref)

    t_idx = lax.broadcasted_iota(jnp.int32, (L, L), 0)
    s_idx = lax.broadcasted_iota(jnp.int32, (L, L), 1)
    m_prev = m_ref[0:1, :]
    ones_ext = jnp.ones((L, LANES), MXU_DTYPE)
    m_new_rows = []

    for direction, (q_ref, k_ref, v_ref, gi_ref, gf_ref, h_ref) in enumerate((
            (qf_ref, kf_ref, vf_ref, gif_ref, gff_ref, hf_ref),
            (qb_ref, kb_ref, vb_ref, gib_ref, gfb_ref, hb_ref))):
        reverse = direction == 1
        mask = (s_idx >= t_idx) if reverse else (s_idx <= t_idx)
        last = 0 if reverse else L - 1
        b = _scan_rows(gf_ref[...], reverse, jnp.add, 0.0)
        a = gi_ref[...] - b
        big_m = jnp.maximum(_scan_rows(a, reverse, jnp.maximum, -jnp.inf), m_prev)
        w_inter = jnp.exp(m_prev - big_m)
        exp_neg_m = jnp.exp(-(b + big_m))
        m_last = big_m[last:last + 1, :]
        decay = jnp.exp(m_prev - m_last)
        w_state = jnp.exp(a - m_last)
        m_new_rows.append(b[last:last + 1, :] + m_last)
        a_t = a.T

        for h in range(N_HEADS):
            lane = direction * N_HEADS + h
            cols = slice(h * HEAD_DIM, (h + 1) * HEAD_DIM)
            q = q_ref[:, cols]
            k = k_ref[:, cols]
            v_ext = jnp.concatenate([v_ref[:, cols], ones_ext], axis=1)
            s_qk = lax.dot_general(q, k, (((1,), (1,)), ((), ())), preferred_element_type=F32)
            logw = jnp.where(mask, a_t[lane:lane + 1, :] - big_m[:, lane:lane + 1], -jnp.inf)
            s = (s_qk * jnp.exp(logw)).astype(MXU_DTYPE)
            c_old = c_ref[lane]
            r = _mm(s, v_ext) + w_inter[:, lane:lane + 1] * _mm(q, c_old.astype(MXU_DTYPE))
            den = jnp.maximum(jnp.abs(r[:, HEAD_DIM:]), exp_neg_m[:, lane:lane + 1])
            inv = 1.0 / den
            h_ref[:, cols] = r[:, :HEAD_DIM] * jnp.concatenate([inv] * (HEAD_DIM // LANES), axis=1)
            kw = (k.astype(F32) * w_state[:, lane:lane + 1]).astype(MXU_DTYPE)
            upd = lax.dot_general(kw, v_ext, (((0,), (0,)), ((), ())), preferred_element_type=F32)
            c_ref[lane] = decay[:, lane:lane + 1] * c_old + upd

    lane_idx = lax.broadcasted_iota(jnp.int32, (1, LANES), 1)
    m_ref[0:1, :] = jnp.where(lane_idx < N_HEADS, m_new_rows[0], m_new_rows[1])


def _mlstm(q, k, v, gi, gf, batch, seq):
    n = q.shape[0]
    nc = seq // CHUNK
    fwd = lambda w: pl.BlockSpec((CHUNK, w), lambda b, c: (b * nc + c, 0))
    bwd = lambda w: pl.BlockSpec((CHUNK, w), lambda b, c: (b * nc + nc - 1 - c, 0))
    return pl.pallas_call(
        _mlstm_kernel,
        grid=(batch, nc),
        in_specs=[fwd(D_MLSTM), fwd(D_MLSTM), fwd(D_MLSTM), fwd(LANES), fwd(LANES),
                  bwd(D_MLSTM), bwd(D_MLSTM), bwd(D_MLSTM), bwd(LANES), bwd(LANES)],
        out_specs=(fwd(D_MLSTM), bwd(D_MLSTM)),
        out_shape=(jax.ShapeDtypeStruct((n, D_MLSTM), F32), jax.ShapeDtypeStruct((n, D_MLSTM), F32)),
        scratch_shapes=[pltpu.VMEM((2 * N_HEADS, HEAD_DIM, HEAD_DIM + LANES), F32),
                        pltpu.VMEM((SUBLANES, LANES), F32)],
        compiler_params=pltpu.CompilerParams(
            dimension_semantics=("parallel", "arbitrary"), vmem_limit_bytes=VMEM_LIMIT),
        name="mlstm_scan",
    )(q, k, v, gi, gf, q, k, v, gi, gf)


def _outproj_kernel(hf_ref, hb_ref, o_ref, u_ref, up_ref, un_ref, x_ref,
                    mhg_ref, wp_ref, ps_ref, wo_ref, bo_ref, g_ref, b_ref,
                    y_ref, uext_ref, psum_ref, *, tiles_per_seq, seq, alpha):
    tm = x_ref.shape[0]
    t_in_seq = pl.program_id(0) % tiles_per_seq
    not_first = (t_in_seq > 0).astype(F32)
    not_last = (t_in_seq < tiles_per_seq - 1).astype(F32)

    uext_ref[0:HALO, :] = up_ref[...] * not_first
    uext_ref[HALO:HALO + tm, :] = u_ref[...]
    uext_ref[HALO + tm:, :] = un_ref[...] * not_last

    for r0 in range(0, tm, ROW_CHUNK):
        rows = slice(r0, r0 + ROW_CHUNK)
        hs = hf_ref[rows, :] + hb_ref[rows, :]
        parts = []
        for h in range(N_HEADS):
            xh = hs[:, h * HEAD_DIM:(h + 1) * HEAD_DIM]
            mu = jnp.mean(xh, axis=-1, keepdims=True)
            d = xh - mu
            var = jnp.mean(d * d, axis=-1, keepdims=True)
            parts.append(d * lax.rsqrt(var + LN_EPS))
        hm = jnp.concatenate(parts, axis=1) * mhg_ref[...] * jax.nn.sigmoid(o_ref[rows, :])

        pos = t_in_seq * tm + r0 + lax.broadcasted_iota(jnp.int32, (ROW_CHUNK, 1), 0)
        base = HALO + r0
        hp_parts = []
        for gi, w in enumerate(POOL_WINDOWS):
            cols = slice(gi * POOL_GROUP_DIM, (gi + 1) * POOL_GROUP_DIM)
            lo = base - w // 2
            n_rows = ROW_CHUNK + w - 2
            acc = uext_ref[lo:lo + n_rows, cols] + uext_ref[lo + 1:lo + 1 + n_rows, cols]
            span = 2
            while span < w:
                psum_ref[0:n_rows, cols] = acc
                n_rows -= span
                acc = psum_ref[0:n_rows, cols] + psum_ref[span:span + n_rows, cols]
                span *= 2
            cnt = (jnp.minimum(pos + (w - w // 2), seq) - jnp.maximum(pos - w // 2, 0)).astype(F32)
            pooled = acc / cnt - uext_ref[base:base + ROW_CHUNK, cols]
            hp_parts.append(_mm(pooled.astype(MXU_DTYPE), wp_ref[gi]))
        hp = jnp.concatenate(hp_parts, axis=1) * ps_ref[...]

        mixed = jnp.concatenate([hm, hp], axis=1).astype(MXU_DTYPE)
        mix = _mm(mixed, wo_ref[...]) + bo_ref[...]
        y_ref[rows, :] = _layer_norm_rows(alpha * x_ref[rows, :] + mix, g_ref[...], b_ref[...])


def _outproj(hf, hb, o, u, x2d, mh_g, w_pool, pool_scale, w_out, b_out, ln_g, ln_b, seq, tm, alpha):
    n = x2d.shape[0]
    tiles_per_seq = seq // tm
    hb_per_tile = tm // HALO
    n_halo_blocks = n // HALO
    row = lambda w: pl.BlockSpec((tm, w), lambda i: (i, 0))
    prev = pl.BlockSpec((HALO, D_POOL), lambda i: (jnp.maximum(i * hb_per_tile - 1, 0), 0))
    nxt = pl.BlockSpec((HALO, D_POOL),
                       lambda i: (jnp.minimum((i + 1) * hb_per_tile, n_halo_blocks - 1), 0))
    kern = functools.partial(_outproj_kernel, tiles_per_seq=tiles_per_seq, seq=seq, alpha=alpha)
    return pl.pallas_call(
        kern,
        grid=(n // tm,),
        in_specs=[row(D_MLSTM), row(D_MLSTM), row(D_MLSTM), row(D_POOL), prev, nxt, row(D_MODEL),
                  _const_spec((1, D_MLSTM)),
                  _const_spec((len(POOL_WINDOWS), POOL_GROUP_DIM, POOL_GROUP_DIM)),
                  _const_spec((1, D_POOL)), _const_spec((D_MODEL, D_MODEL)),
                  _const_spec((1, D_MODEL)), _const_spec((1, D_MODEL)), _const_spec((1, D_MODEL))],
        out_specs=row(D_MODEL),
        out_shape=jax.ShapeDtypeStruct((n, D_MODEL), F32),
        scratch_shapes=[pltpu.VMEM((tm + 2 * HALO, D_POOL), F32),
                        pltpu.VMEM((ROW_CHUNK + 2 * HALO, D_POOL), F32)],
        compiler_params=pltpu.CompilerParams(
            dimension_semantics=("parallel",), vmem_limit_bytes=VMEM_LIMIT),
        name="outproj_ln",
    )(hf, hb, o, u, u, u, x2d, mh_g, w_pool, pool_scale, w_out, b_out, ln_g, ln_b)


def _ffn_kernel(x_ref, xp_ref, xn_ref, wg_ref, wv_ref, vg_ref, vv_ref, wd_ref, bd_ref, g_ref, b_ref,
                y_ref, xs_ref, a_ref, *, tiles_per_seq, alpha):
    tm = x_ref.shape[0]
    tf = wd_ref.shape[0]
    j = pl.program_id(1)
    t_in_seq = pl.program_id(0) % tiles_per_seq
    not_first = (t_in_seq > 0).astype(F32)
    not_last = (t_in_seq < tiles_per_seq - 1).astype(F32)

    @pl.when(j == 0)
    def _():
        x = x_ref[...]
        xs_ref[...] = jnp.concatenate([xp_ref[...], x, xn_ref[...]], axis=0).astype(xs_ref.dtype)
        y_ref[...] = alpha * x + bd_ref[...]

    xs = xs_ref[...]
    conv = []
    for part, (w_ref, vec_ref) in enumerate(((wg_ref, vg_ref), (wv_ref, vv_ref))):
        cols = slice(part * tf, (part + 1) * tf)
        a = _mm(xs, w_ref[...])
        outside = -vec_ref[0:1, :]
        a_ref[0:HALO, cols] = a[0:HALO] * not_first + outside * (1.0 - not_first)
        a_ref[HALO:HALO + tm, cols] = a[HALO:HALO + tm]
        a_ref[HALO + tm:, cols] = a[HALO + tm:] * not_last + outside * (1.0 - not_last)
        conv.append(vec_ref[1:2, :] * a_ref[HALO - 1:HALO - 1 + tm, cols]
                    + vec_ref[2:3, :] * a_ref[HALO:HALO + tm, cols]
                    + vec_ref[3:4, :] * a_ref[HALO + 1:HALO + 1 + tm, cols] + vec_ref[4:5, :])
    gate, val = conv
    f = (gate * jax.nn.sigmoid(gate) * val).astype(MXU_DTYPE)
    for cols in (slice(0, D_MODEL // 2), slice(D_MODEL // 2, D_MODEL)):
        y_ref[:, cols] += _mm(f, wd_ref[:, cols])

    @pl.when(j == pl.num_programs(1) - 1)
    def _():
        y_ref[...] = _layer_norm_rows(y_ref[...], g_ref[...], b_ref[...])


def _ffn(x1, w_up, ff_vecs, w_down, b_down, ln_g, ln_b, seq, tm, tf, alpha):
    n = x1.shape[0]
    tiles_per_seq = seq // tm
    hb_per_tile = tm // HALO
    n_halo_blocks = n // HALO
    nj = D_FF // tf
    main = pl.BlockSpec((tm, D_MODEL), lambda i, j: (i, 0))
    prev = pl.BlockSpec((HALO, D_MODEL), lambda i, j: (jnp.maximum(i * hb_per_tile - 1, 0), 0))
    nxt = pl.BlockSpec((HALO, D_MODEL),
                       lambda i, j: (jnp.minimum((i + 1) * hb_per_tile, n_halo_blocks - 1), 0))
    up_gate = pl.BlockSpec((D_MODEL, tf), lambda i, j: (0, j))
    up_val = pl.BlockSpec((D_MODEL, tf), lambda i, j: (0, j + nj))
    vec_gate = pl.BlockSpec((SUBLANES, tf), lambda i, j: (0, j))
    vec_val = pl.BlockSpec((SUBLANES, tf), lambda i, j: (0, j + nj))
    down = pl.BlockSpec((tf, D_MODEL), lambda i, j: (j, 0))
    vec = pl.BlockSpec((1, D_MODEL), lambda i, j: (0, 0))
    kern = functools.partial(_ffn_kernel, tiles_per_seq=tiles_per_seq, alpha=alpha)
    return pl.pallas_call(
        kern,
        grid=(n // tm, nj),
        in_specs=[main, prev, nxt, up_gate, up_val, vec_gate, vec_val, down, vec, vec, vec],
        out_specs=main,
        out_shape=jax.ShapeDtypeStruct((n, D_MODEL), F32),
        scratch_shapes=[pltpu.VMEM((tm + 2 * HALO, D_MODEL), MXU_DTYPE),
                        pltpu.VMEM((tm + 2 * HALO, 2 * tf), F32)],
        compiler_params=pltpu.CompilerParams(
            dimension_semantics=("parallel", "arbitrary"), vmem_limit_bytes=VMEM_LIMIT),
        name="ffn_ln",
    )(x1, x1, x1, w_up, w_up, ff_vecs, ff_vecs, w_down, b_down, ln_g, ln_b)


def _pack_layer(w_in, b_in, mh_norm_g, w_pool, pool_scale, w_out, b_out, ln1_g, ln1_b,
                w_up, b_up, w_conv, b_conv, w_down, b_down, ln2_g, ln2_b):
    g0 = 4 * D_MLSTM
    h = N_HEADS
    ng = 4 * h

    def gate_cols(a):
        gts = a[..., g0:g0 + ng]
        pad = jnp.zeros(a.shape[:-1] + (LANES - 2 * h,), a.dtype)
        return jnp.concatenate([gts[..., 0:h], gts[..., 2 * h:3 * h], pad,
                                gts[..., h:2 * h], gts[..., 3 * h:4 * h], pad], axis=-1)

    pack_in = lambda a: jnp.concatenate([a[..., :g0], a[..., g0 + ng:], gate_cols(a)], axis=-1)
    row = lambda a: a.reshape(1, -1).astype(F32)

    return dict(
        w_qkvo=w_in[:, :g0].astype(MXU_DTYPE), w_u=w_in[:, g0 + ng:].astype(MXU_DTYPE),
        w_gates=gate_cols(w_in).astype(MXU_DTYPE), b_in=row(pack_in(b_in)),
        mh_g=row(mh_norm_g), w_pool=w_pool.astype(MXU_DTYPE), pool_scale=row(pool_scale),
        w_out=w_out.astype(MXU_DTYPE), b_out=row(b_out), ln1_g=row(ln1_g), ln1_b=row(ln1_b),
        w_up=w_up.astype(MXU_DTYPE),
        ff_vecs=jnp.concatenate(
            [b_up[None], w_conv, (b_conv + b_up * jnp.sum(w_conv, axis=0))[None],
             jnp.zeros((SUBLANES - 5, 2 * D_FF), F32)], axis=0).astype(F32),
        w_down=w_down.astype(MXU_DTYPE), b_down=row(b_down), ln2_g=row(ln2_g), ln2_b=row(ln2_b),
    )


def _tile(seq, want):
    return min(want, seq)


def _layer(x2d, p, batch, seq, alpha):
    q, k, v, o, u, gi, gf = _inproj(x2d, p["w_qkvo"], p["w_u"], p["w_gates"], p["b_in"],
                                    _tile(seq, 512))
    hf, hb = _mlstm(q, k, v, gi, gf, batch, seq)
    x1 = _outproj(hf, hb, o, u, x2d, p["mh_g"], p["w_pool"], p["pool_scale"], p["w_out"], p["b_out"],
                  p["ln1_g"], p["ln1_b"], seq, _tile(seq, 512), alpha)
    return _ffn(x1, p["w_up"], p["ff_vecs"], p["w_down"], p["b_down"], p["ln2_g"], p["ln2_b"],
                seq, _tile(seq, 512), FF_TILE, alpha)


def kernel(x_prompt, x_sample, w_in, b_in, mh_norm_g, w_pool, pool_scale, w_out, b_out, ln1_g, ln1_b,
           w_up, b_up, w_conv, b_conv, w_down, b_down, ln2_g, ln2_b):
    weights = (w_in, b_in, mh_norm_g, w_pool, pool_scale, w_out, b_out, ln1_g, ln1_b,
               w_up, b_up, w_conv, b_conv, w_down, b_down, ln2_g, ln2_b)
    depth = w_in.shape[0]
    alpha = (2.0 * depth) ** 0.25
    layers = [_pack_layer(*(w[l] for w in weights)) for l in range(depth)]

    def trunk(x):
        batch, seq, _ = x.shape
        x2d = x.reshape(batch * seq, D_MODEL)
        for p in layers:
            x2d = _layer(x2d, p, batch, seq, alpha)
        return x2d.reshape(batch, seq, D_MODEL)

    return trunk(x_prompt), trunk(x_sample)
```

```python
import functools

import jax
import jax.numpy as jnp
from jax import lax
from jax.experimental import pallas as pl
from jax.experimental.pallas import tpu as pltpu

D_MODEL = 2048
D_MLSTM = 1024
N_HEADS = 4
HEAD_DIM = D_MLSTM // N_HEADS
D_POOL = D_MODEL - D_MLSTM
POOL_WINDOWS = (2, 4, 8, 16)
POOL_GROUP_DIM = D_POOL // len(POOL_WINDOWS)
D_FF = 5632
LN_EPS = 1e-5

LANES = 128
SUBLANES = 8
HALO = SUBLANES
CHUNK = 256
GATE_COLS = 2 * LANES
D_IN_PACKED = 4 * D_MLSTM + D_POOL + GATE_COLS
FF_TILE = 512
ROW_CHUNK = 128
VMEM_LIMIT = 56 * 1024 * 1024

MXU_DTYPE = jnp.bfloat16
F32 = jnp.float32


def _mm(a, b):
    return jnp.dot(a, b, preferred_element_type=F32)


def _layer_norm_rows(y, g, b):
    mu = jnp.mean(y, axis=-1, keepdims=True)
    d = y - mu
    var = jnp.mean(d * d, axis=-1, keepdims=True)
    return d * lax.rsqrt(var + LN_EPS) * g + b


def _const_spec(shape):
    nd = len(shape)
    return pl.BlockSpec(shape, lambda *_: (0,) * nd, pipeline_mode=pl.Buffered(1))


def _inproj_kernel(x_ref, w_ref, wu_ref, wg_ref, b_ref,
                   q_ref, k_ref, v_ref, o_ref, u_ref, gi_ref, gf_ref):
    xb = x_ref[...].astype(MXU_DTYPE)
    d = D_MLSTM

    def proj(c0, c1):
        return _mm(xb, w_ref[:, c0:c1]) + b_ref[:, c0:c1]

    q_ref[...] = proj(0, d).astype(q_ref.dtype)
    k_ref[...] = (proj(d, 2 * d) * (HEAD_DIM ** -0.5)).astype(k_ref.dtype)
    v_ref[...] = proj(2 * d, 3 * d).astype(v_ref.dtype)
    o_ref[...] = proj(3 * d, 4 * d)
    u_ref[...] = _mm(xb, wu_ref[...]) + b_ref[:, 4 * d:4 * d + D_POOL]
    g = _mm(xb, wg_ref[...]) + b_ref[:, 4 * d + D_POOL:]
    gi_ref[...] = g[:, :LANES]
    fp = g[:, LANES:]
    gf_ref[...] = jnp.minimum(fp, 0.0) - jnp.log1p(jnp.exp(-jnp.abs(fp)))


def _inproj(x2d, w_qkvo, w_u, w_gates, b_packed, tm):
    n = x2d.shape[0]
    row = lambda w: pl.BlockSpec((tm, w), lambda i: (i, 0))
    out_shape = (
        jax.ShapeDtypeStruct((n, D_MLSTM), MXU_DTYPE),
        jax.ShapeDtypeStruct((n, D_MLSTM), MXU_DTYPE),
        jax.ShapeDtypeStruct((n, D_MLSTM), MXU_DTYPE),
        jax.ShapeDtypeStruct((n, D_MLSTM), F32),
        jax.ShapeDtypeStruct((n, D_POOL), F32),
        jax.ShapeDtypeStruct((n, LANES), F32),
        jax.ShapeDtypeStruct((n, LANES), F32),
    )
    return pl.pallas_call(
        _inproj_kernel,
        grid=(n // tm,),
        in_specs=[row(D_MODEL), _const_spec((D_MODEL, 4 * D_MLSTM)), _const_spec((D_MODEL, D_POOL)),
                  _const_spec((D_MODEL, GATE_COLS)), _const_spec((1, D_IN_PACKED))],
        out_specs=(row(D_MLSTM), row(D_MLSTM), row(D_MLSTM), row(D_MLSTM), row(D_POOL),
                   row(LANES), row(LANES)),
        out_shape=out_shape,
        compiler_params=pltpu.CompilerParams(
            dimension_semantics=("parallel",), vmem_limit_bytes=VMEM_LIMIT),
        name="inproj",
    )(x2d, w_qkvo, w_u, w_gates, b_packed)


def _scan_rows(x, reverse, combine, identity):
    n = x.shape[0]
    ridx = lax.broadcasted_iota(jnp.int32, x.shape, 0)
    sh = 1
    while sh < n:
        if reverse:
            shifted = pltpu.roll(x, n - sh, axis=0)
            valid = ridx < n - sh
        else:
            shifted = pltpu.roll(x, sh, axis=0)
            valid = ridx >= sh
        x = combine(x, jnp.where(valid, shifted, identity))
        sh *= 2
    return x


def _mlstm_kernel(qf_ref, kf_ref, vf_ref, gif_ref, gff_ref,
                  qb_ref, kb_ref, vb_ref, gib_ref, gfb_ref,
                  hf_ref, hb_ref, c_ref, m_ref):
    L = CHUNK

    @pl.when(pl.program_id(1) == 0)
    def _():
        c_ref[...] = jnp.zeros_like(c_ref)
        m_ref[...] = jnp.zeros_like(m_ref)

    t_idx = lax.broadcasted_iota(jnp.int32, (L, L), 0)
    s_idx = lax.broadcasted_iota(jnp.int32, (L, L), 1)
    m_prev = m_ref[0:1, :]
    ones_ext = jnp.ones((L, LANES), MXU_DTYPE)
    m_new_rows = []

    for direction, (q_ref, k_ref, v_ref, gi_ref, gf_ref, h_ref) in enumerate((
            (qf_ref, kf_ref, vf_ref, gif_ref, gff_ref, hf_ref),
            (qb_ref, kb_ref, vb_ref, gib_ref, gfb_ref, hb_ref))):
        reverse = direction == 1
        mask = (s_idx >= t_idx) if reverse else (s_idx <= t_idx)
        last = 0 if reverse else L - 1
        b = _scan_rows(gf_ref[...], reverse, jnp.add, 0.0)
        a = gi_ref[...] - b
        big_m = jnp.maximum(_scan_rows(a, reverse, jnp.maximum, -jnp.inf), m_prev)
        w_inter = jnp.exp(m_prev - big_m)
        exp_neg_m = jnp.exp(-(b + big_m))
        m_last = big_m[last:last + 1, :]
        decay = jnp.exp(m_prev - m_last)
        w_state = jnp.exp(a - m_last)
        m_new_rows.append(b[last:last + 1, :] + m_last)
        a_t = a.T

        for h in range(N_HEADS):
            lane = direction * N_HEADS + h
            cols = slice(h * HEAD_DIM, (h + 1) * HEAD_DIM)
            q = q_ref[:, cols]
            k = k_ref[:, cols]
            v_ext = jnp.concatenate([v_ref[:, cols], ones_ext], axis=1)
            s_qk = lax.dot_general(q, k, (((1,), (1,)), ((), ())), preferred_element_type=F32)
            logw = jnp.where(mask, a_t[lane:lane + 1, :] - big_m[:, lane:lane + 1], -jnp.inf)
            s = (s_qk * jnp.exp(logw)).astype(MXU_DTYPE)
            c_old = c_ref[lane]
            r = _mm(s, v_ext) + w_inter[:, lane:lane + 1] * _mm(q, c_old.astype(MXU_DTYPE))
            den = jnp.maximum(jnp.abs(r[:, HEAD_DIM:]), exp_neg_m[:, lane:lane + 1])
            inv = 1.0 / den
            h_ref[:, cols] = r[:, :HEAD_DIM] * jnp.concatenate([inv] * (HEAD_DIM // LANES), axis=1)
            kw = (k.astype(F32) * w_state[:, lane:lane + 1]).astype(MXU_DTYPE)
            upd = lax.dot_general(kw, v_ext, (((0,), (0,)), ((), ())), preferred_element_type=F32)
            c_ref[lane] = decay[:, lane:lane + 1] * c_old + upd

    lane_idx = lax.broadcasted_iota(jnp.int32, (1, LANES), 1)
    m_ref[0:1, :] = jnp.where(lane_idx < N_HEADS, m_new_rows[0], m_new_rows[1])


def _mlstm(q, k, v, gi, gf, batch, seq):
    n = q.shape[0]
    nc = seq // CHUNK
    fwd = lambda w: pl.BlockSpec((CHUNK, w), lambda b, c: (b * nc + c, 0))
    bwd = lambda w: pl.BlockSpec((CHUNK, w), lambda b, c: (b * nc + nc - 1 - c, 0))
    return pl.pallas_call(
        _mlstm_kernel,
        grid=(batch, nc),
        in_specs=[fwd(D_MLSTM), fwd(D_MLSTM), fwd(D_MLSTM), fwd(LANES), fwd(LANES),
                  bwd(D_MLSTM), bwd(D_MLSTM), bwd(D_MLSTM), bwd(LANES), bwd(LANES)],
        out_specs=(fwd(D_MLSTM), bwd(D_MLSTM)),
        out_shape=(jax.ShapeDtypeStruct((n, D_MLSTM), F32), jax.ShapeDtypeStruct((n, D_MLSTM), F32)),
        scratch_shapes=[pltpu.VMEM((2 * N_HEADS, HEAD_DIM, HEAD_DIM + LANES), F32),
                        pltpu.VMEM((SUBLANES, LANES), F32)],
        compiler_params=pltpu.CompilerParams(
            dimension_semantics=("parallel", "arbitrary"), vmem_limit_bytes=VMEM_LIMIT),
        name="mlstm_scan",
    )(q, k, v, gi, gf, q, k, v, gi, gf)


def _outproj_kernel(hf_ref, hb_ref, o_ref, u_ref, up_ref, un_ref, x_ref,
                    mhg_ref, wp_ref, ps_ref, wo_ref, bo_ref, g_ref, b_ref,
                    y_ref, uext_ref, psum_ref, *, tiles_per_seq, seq, alpha):
    tm = x_ref.shape[0]
    t_in_seq = pl.program_id(0) % tiles_per_seq
    not_first = (t_in_seq > 0).astype(F32)
    not_last = (t_in_seq < tiles_per_seq - 1).astype(F32)

    uext_ref[0:HALO, :] = up_ref[...] * not_first
    uext_ref[HALO:HALO + tm, :] = u_ref[...]
    uext_ref[HALO + tm:, :] = un_ref[...] * not_last

    for r0 in range(0, tm, ROW_CHUNK):
        rows = slice(r0, r0 + ROW_CHUNK)
        hs = hf_ref[rows, :] + hb_ref[rows, :]
        parts = []
        for h in range(N_HEADS):
            xh = hs[:, h * HEAD_DIM:(h + 1) * HEAD_DIM]
            mu = jnp.mean(xh, axis=-1, keepdims=True)
            d = xh - mu
            var = jnp.mean(d * d, axis=-1, keepdims=True)
            parts.append(d * lax.rsqrt(var + LN_EPS))
        hm = jnp.concatenate(parts, axis=1) * mhg_ref[...] * jax.nn.sigmoid(o_ref[rows, :])

        pos = t_in_seq * tm + r0 + lax.broadcasted_iota(jnp.int32, (ROW_CHUNK, 1), 0)
        base = HALO + r0
        hp_parts = []
        for gi, w in enumerate(POOL_WINDOWS):
            cols = slice(gi * POOL_GROUP_DIM, (gi + 1) * POOL_GROUP_DIM)
            lo = base - w // 2
            n_rows = ROW_CHUNK + w - 2
            acc = uext_ref[lo:lo + n_rows, cols] + uext_ref[lo + 1:lo + 1 + n_rows, cols]
            span = 2
            while span < w:
                psum_ref[0:n_rows, cols] = acc
                n_rows -= span
                acc = psum_ref[0:n_rows, cols] + psum_ref[span:span + n_rows, cols]
                span *= 2
            cnt = (jnp.minimum(pos + (w - w // 2), seq) - jnp.maximum(pos - w // 2, 0)).astype(F32)
            pooled = acc / cnt - uext_ref[base:base + ROW_CHUNK, cols]
            hp_parts.append(_mm(pooled.astype(MXU_DTYPE), wp_ref[gi]))
        hp = jnp.concatenate(hp_parts, axis=1) * ps_ref[...]

        mixed = jnp.concatenate([hm, hp], axis=1).astype(MXU_DTYPE)
        mix = _mm(mixed, wo_ref[...]) + bo_ref[...]
        y_ref[rows, :] = _layer_norm_rows(alpha * x_ref[rows, :] + mix, g_ref[...], b_ref[...])


def _outproj(hf, hb, o, u, x2d, mh_g, w_pool, pool_scale, w_out, b_out, ln_g, ln_b, seq, tm, alpha):
    n = x2d.shape[0]
    tiles_per_seq = seq // tm
    hb_per_tile = tm // HALO
    n_halo_blocks = n // HALO
    row = lambda w: pl.BlockSpec((tm, w), lambda i: (i, 0))
    prev = pl.BlockSpec((HALO, D_POOL), lambda i: (jnp.maximum(i * hb_per_tile - 1, 0), 0))
    nxt = pl.BlockSpec((HALO, D_POOL),
                       lambda i: (jnp.minimum((i + 1) * hb_per_tile, n_halo_blocks - 1), 0))
    kern = functools.partial(_outproj_kernel, tiles_per_seq=tiles_per_seq, seq=seq, alpha=alpha)
    return pl.pallas_call(
        kern,
        grid=(n // tm,),
        in_specs=[row(D_MLSTM), row(D_MLSTM), row(D_MLSTM), row(D_POOL), prev, nxt, row(D_MODEL),
                  _const_spec((1, D_MLSTM)),
                  _const_spec((len(POOL_WINDOWS), POOL_GROUP_DIM, POOL_GROUP_DIM)),
                  _const_spec((1, D_POOL)), _const_spec((D_MODEL, D_MODEL)),
                  _const_spec((1, D_MODEL)), _const_spec((1, D_MODEL)), _const_spec((1, D_MODEL))],
        out_specs=row(D_MODEL),
        out_shape=jax.ShapeDtypeStruct((n, D_MODEL), F32),
        scratch_shapes=[pltpu.VMEM((tm + 2 * HALO, D_POOL), F32),
                        pltpu.VMEM((ROW_CHUNK + 2 * HALO, D_POOL), F32)],
        compiler_params=pltpu.CompilerParams(
            dimension_semantics=("parallel",), vmem_limit_bytes=VMEM_LIMIT),
        name="outproj_ln",
    )(hf, hb, o, u, u, u, x2d, mh_g, w_pool, pool_scale, w_out, b_out, ln_g, ln_b)


def _ffn_kernel(x_ref, xp_ref, xn_ref, wg_ref, wv_ref, vg_ref, vv_ref, wd_ref, bd_ref, g_ref, b_ref,
                y_ref, xs_ref, a_ref, *, tiles_per_seq, alpha):
    tm = x_ref.shape[0]
    tf = wd_ref.shape[0]
    j = pl.program_id(1)
    t_in_seq = pl.program_id(0) % tiles_per_seq
    not_first = (t_in_seq > 0).astype(F32)
    not_last = (t_in_seq < tiles_per_seq - 1).astype(F32)

    @pl.when(j == 0)
    def _():
        x = x_ref[...]
        xs_ref[...] = jnp.concatenate([xp_ref[...], x, xn_ref[...]], axis=0).astype(xs_ref.dtype)
        y_ref[...] = alpha * x + bd_ref[...]

    xs = xs_ref[...]
    conv = []
    for part, (w_ref, vec_ref) in enumerate(((wg_ref, vg_ref), (wv_ref, vv_ref))):
        cols = slice(part * tf, (part + 1) * tf)
        a = _mm(xs, w_ref[...])
        outside = -vec_ref[0:1, :]
        a_ref[0:HALO, cols] = a[0:HALO] * not_first + outside * (1.0 - not_first)
        a_ref[HALO:HALO + tm, cols] = a[HALO:HALO + tm]
        a_ref[HALO + tm:, cols] = a[HALO + tm:] * not_last + outside * (1.0 - not_last)
        conv.append(vec_ref[1:2, :] * a_ref[HALO - 1:HALO - 1 + tm, cols]
                    + vec_ref[2:3, :] * a_ref[HALO:HALO + tm, cols]
                    + vec_ref[3:4, :] * a_ref[HALO + 1:HALO + 1 + tm, cols] + vec_ref[4:5, :])
    gate, val = conv
    f = (gate * jax.nn.sigmoid(gate) * val).astype(MXU_DTYPE)
    for cols in (slice(0, D_MODEL // 2), slice(D_MODEL // 2, D_MODEL)):
        y_ref[:, cols] += _mm(f, wd_ref[:, cols])

    @pl.when(j == pl.num_programs(1) - 1)
    def _():
        y_ref[...] = _layer_norm_rows(y_ref[...], g_ref[...], b_ref[...])


def _ffn(x1, w_up, ff_vecs, w_down, b_down, ln_g, ln_b, seq, tm, tf, alpha):
    n = x1.shape[0]
    tiles_per_seq = seq // tm
    hb_per_tile = tm // HALO
    n_halo_blocks = n // HALO
    nj = D_FF // tf
    main = pl.BlockSpec((tm, D_MODEL), lambda i, j: (i, 0))
    prev = pl.BlockSpec((HALO, D_MODEL), lambda i, j: (jnp.maximum(i * hb_per_tile - 1, 0), 0))
    nxt = pl.BlockSpec((HALO, D_MODEL),
                       lambda i, j: (jnp.minimum((i + 1) * hb_per_tile, n_halo_blocks - 1), 0))
    up_gate = pl.BlockSpec((D_MODEL, tf), lambda i, j: (0, j))
    up_val = pl.BlockSpec((D_MODEL, tf), lambda i, j: (0, j + nj))
    vec_gate = pl.BlockSpec((SUBLANES, tf), lambda i, j: (0, j))
    vec_val = pl.BlockSpec((SUBLANES, tf), lambda i, j: (0, j + nj))
    down = pl.BlockSpec((tf, D_MODEL), lambda i, j: (j, 0))
    vec = pl.BlockSpec((1, D_MODEL), lambda i, j: (0, 0))
    kern = functools.partial(_ffn_kernel, tiles_per_seq=tiles_per_seq, alpha=alpha)
    return pl.pallas_call(
        kern,
        grid=(n // tm, nj),
        in_specs=[main, prev, nxt, up_gate, up_val, vec_gate, vec_val, down, vec, vec, vec],
        out_specs=main,
        out_shape=jax.ShapeDtypeStruct((n, D_MODEL), F32),
        scratch_shapes=[pltpu.VMEM((tm + 2 * HALO, D_MODEL), MXU_DTYPE),
                        pltpu.VMEM((tm + 2 * HALO, 2 * tf), F32)],
        compiler_params=pltpu.CompilerParams(
            dimension_semantics=("parallel", "arbitrary"), vmem_limit_bytes=VMEM_LIMIT),
        name="ffn_ln",
    )(x1, x1, x1, w_up, w_up, ff_vecs, ff_vecs, w_down, b_down, ln_g, ln_b)


def _pack_layer(w_in, b_in, mh_norm_g, w_pool, pool_scale, w_out, b_out, ln1_g, ln1_b,
                w_up, b_up, w_conv, b_conv, w_down, b_down, ln2_g, ln2_b):
    g0 = 4 * D_MLSTM
    h = N_HEADS
    ng = 4 * h

    def gate_cols(a):
        gts = a[..., g0:g0 + ng]
        pad = jnp.zeros(a.shape[:-1] + (LANES - 2 * h,), a.dtype)
        return jnp.concatenate([gts[..., 0:h], gts[..., 2 * h:3 * h], pad,
                                gts[..., h:2 * h], gts[..., 3 * h:4 * h], pad], axis=-1)

    pack_in = lambda a: jnp.concatenate([a[..., :g0], a[..., g0 + ng:], gate_cols(a)], axis=-1)
    row = lambda a: a.reshape(1, -1).astype(F32)
    w_in_mxu = w_in.astype(MXU_DTYPE)

    return dict(
        w_qkvo=w_in_mxu[:, :g0], w_u=w_in_mxu[:, g0 + ng:],
        w_gates=gate_cols(w_in_mxu), b_in=row(pack_in(b_in)),
        mh_g=row(mh_norm_g), w_pool=w_pool.astype(MXU_DTYPE), pool_scale=row(pool_scale),
        w_out=w_out.astype(MXU_DTYPE), b_out=row(b_out), ln1_g=row(ln1_g), ln1_b=row(ln1_b),
        w_up=w_up.astype(MXU_DTYPE),
        ff_vecs=jnp.concatenate(
            [b_up[None], w_conv, (b_conv + b_up * jnp.sum(w_conv, axis=0))[None],
             jnp.zeros((SUBLANES - 5, 2 * D_FF), F32)], axis=0).astype(F32),
        w_down=w_down.astype(MXU_DTYPE), b_down=row(b_down), ln2_g=row(ln2_g), ln2_b=row(ln2_b),
    )


def _tile(seq, want):
    return min(want, seq)


def _layer(x2d, p, batch, seq, alpha):
    q, k, v, o, u, gi, gf = _inproj(x2d, p["w_qkvo"], p["w_u"], p["w_gates"], p["b_in"],
                                    _tile(seq, 512))
    hf, hb = _mlstm(q, k, v, gi, gf, batch, seq)
    x1 = _outproj(hf, hb, o, u, x2d, p["mh_g"], p["w_pool"], p["pool_scale"], p["w_out"], p["b_out"],
                  p["ln1_g"], p["ln1_b"], seq, _tile(seq, 512), alpha)
    return _ffn(x1, p["w_up"], p["ff_vecs"], p["w_down"], p["b_down"], p["ln2_g"], p["ln2_b"],
                seq, _tile(seq, 512), FF_TILE, alpha)


def kernel(x_prompt, x_sample, w_in, b_in, mh_norm_g, w_pool, pool_scale, w_out, b_out, ln1_g, ln1_b,
           w_up, b_up, w_conv, b_conv, w_down, b_down, ln2_g, ln2_b):
    weights = (w_in, b_in, mh_norm_g, w_pool, pool_scale, w_out, b_out, ln1_g, ln1_b,
               w_up, b_up, w_conv, b_conv, w_down, b_down, ln2_g, ln2_b)
    depth = w_in.shape[0]
    alpha = (2.0 * depth) ** 0.25
    layers = [_pack_layer(*(w[l] for w in weights)) for l in range(depth)]

    def trunk(x):
        batch, seq, _ = x.shape
        x2d = x.reshape(batch * seq, D_MODEL)
        for p in layers:
            x2d = _layer(x2d, p, batch, seq, alpha)
        return x2d.reshape(batch, seq, D_MODEL)

    return trunk(x_prompt), trunk(x_sample)
```

```python
import functools

import jax
import jax.numpy as jnp
from jax import lax
from jax.experimental import pallas as pl
from jax.experimental.pallas import tpu as pltpu

D_MODEL = 2048
D_MLSTM = 1024
N_HEADS = 4
HEAD_DIM = D_MLSTM // N_HEADS
D_POOL = D_MODEL - D_MLSTM
POOL_WINDOWS = (2, 4, 8, 16)
POOL_GROUP_DIM = D_POOL // len(POOL_WINDOWS)
D_FF = 5632
LN_EPS = 1e-5

LANES = 128
SUBLANES = 8
HALO = SUBLANES
CHUNK = 256
GATE_COLS = 2 * LANES
D_IN_PACKED = 4 * D_MLSTM + D_POOL + GATE_COLS
FF_TILE = 512
ROW_CHUNK = 128
VMEM_LIMIT = 56 * 1024 * 1024

MXU_DTYPE = jnp.bfloat16
F32 = jnp.float32


def _mm(a, b):
    return jnp.dot(a, b, preferred_element_type=F32)


def _layer_norm_rows(y, g, b):
    mu = jnp.mean(y, axis=-1, keepdims=True)
    d = y - mu
    var = jnp.mean(d * d, axis=-1, keepdims=True)
    return d * lax.rsqrt(var + LN_EPS) * g + b


def _const_spec(shape):
    nd = len(shape)
    return pl.BlockSpec(shape, lambda *_: (0,) * nd, pipeline_mode=pl.Buffered(1))


def _inproj_kernel(x_ref, w_ref, wu_ref, wg_ref, b_ref,
                   q_ref, k_ref, v_ref, o_ref, u_ref, gi_ref, gf_ref):
    xb = x_ref[...].astype(MXU_DTYPE)
    d = D_MLSTM

    def proj(c0, c1):
        return _mm(xb, w_ref[:, c0:c1]) + b_ref[:, c0:c1]

    q_ref[...] = proj(0, d).astype(q_ref.dtype)
    k_ref[...] = (proj(d, 2 * d) * (HEAD_DIM ** -0.5)).astype(k_ref.dtype)
    v_ref[...] = proj(2 * d, 3 * d).astype(v_ref.dtype)
    o_ref[...] = proj(3 * d, 4 * d)
    u_ref[...] = _mm(xb, wu_ref[...]) + b_ref[:, 4 * d:4 * d + D_POOL]
    g = _mm(xb, wg_ref[...]) + b_ref[:, 4 * d + D_POOL:]
    gi_ref[...] = g[:, :LANES]
    fp = g[:, LANES:]
    gf_ref[...] = jnp.minimum(fp, 0.0) - jnp.log1p(jnp.exp(-jnp.abs(fp)))


def _inproj(x2d, w_qkvo, w_u, w_gates, b_packed, tm):
    n = x2d.shape[0]
    row = lambda w: pl.BlockSpec((tm, w), lambda i: (i, 0))
    out_shape = (
        jax.ShapeDtypeStruct((n, D_MLSTM), MXU_DTYPE),
        jax.ShapeDtypeStruct((n, D_MLSTM), MXU_DTYPE),
        jax.ShapeDtypeStruct((n, D_MLSTM), MXU_DTYPE),
        jax.ShapeDtypeStruct((n, D_MLSTM), F32),
        jax.ShapeDtypeStruct((n, D_POOL), F32),
        jax.ShapeDtypeStruct((n, LANES), F32),
        jax.ShapeDtypeStruct((n, LANES), F32),
    )
    return pl.pallas_call(
        _inproj_kernel,
        grid=(n // tm,),
        in_specs=[row(D_MODEL), _const_spec((D_MODEL, 4 * D_MLSTM)), _const_spec((D_MODEL, D_POOL)),
                  _const_spec((D_MODEL, GATE_COLS)), _const_spec((1, D_IN_PACKED))],
        out_specs=(row(D_MLSTM), row(D_MLSTM), row(D_MLSTM), row(D_MLSTM), row(D_POOL),
                   row(LANES), row(LANES)),
        out_shape=out_shape,
        compiler_params=pltpu.CompilerParams(
            dimension_semantics=("parallel",), vmem_limit_bytes=VMEM_LIMIT),
        name="inproj",
    )(x2d, w_qkvo, w_u, w_gates, b_packed)


def _scan_rows(x, reverse, combine, identity):
    n = x.shape[0]
    ridx = lax.broadcasted_iota(jnp.int32, x.shape, 0)
    sh = 1
    while sh < n:
        if reverse:
            shifted = pltpu.roll(x, n - sh, axis=0)
            valid = ridx < n - sh
        else:
            shifted = pltpu.roll(x, sh, axis=0)
            valid = ridx >= sh
        x = combine(x, jnp.where(valid, shifted, identity))
        sh *= 2
    return x


def _mlstm_kernel(qf_ref, kf_ref, vf_ref, gif_ref, gff_ref,
                  qb_ref, kb_ref, vb_ref, gib_ref, gfb_ref,
                  hf_ref, hb_ref, c_ref, m_ref):
    L = CHUNK

    @pl.when(pl.program_id(1) == 0)
    def _():
        c_ref[...] = jnp.zeros_like(c_ref)
        m_ref[...] = jnp.zeros_like(m_ref)

    t_idx = lax.broadcasted_iota(jnp.int32, (L, L), 0)
    s_idx = lax.broadcasted_iota(jnp.int32, (L, L), 1)
    m_prev = m_ref[0:1, :]
    ones_ext = jnp.ones((L, LANES), MXU_DTYPE)
    m_new_rows = []

    for direction, (q_ref, k_ref, v_ref, gi_ref, gf_ref, h_ref) in enumerate((
            (qf_ref, kf_ref, vf_ref, gif_ref, gff_ref, hf_ref),
            (qb_ref, kb_ref, vb_ref, gib_ref, gfb_ref, hb_ref))):
        reverse = direction == 1
        mask = (s_idx >= t_idx) if reverse else (s_idx <= t_idx)
        last = 0 if reverse else L - 1
        b = _scan_rows(gf_ref[...], reverse, jnp.add, 0.0)
        a = gi_ref[...] - b
        big_m = jnp.maximum(_scan_rows(a, reverse, jnp.maximum, -jnp.inf), m_prev)
        w_inter = jnp.exp(m_prev - big_m)
        exp_neg_m = jnp.exp(-(b + big_m))
        m_last = big_m[last:last + 1, :]
        decay = jnp.exp(m_prev - m_last)
        w_state = jnp.exp(a - m_last)
        m_new_rows.append(b[last:last + 1, :] + m_last)
        a_t = a.T

        for h in range(N_HEADS):
            lane = direction * N_HEADS + h
            cols = slice(h * HEAD_DIM, (h + 1) * HEAD_DIM)
            q = q_ref[:, cols]
            k = k_ref[:, cols]
            v_ext = jnp.concatenate([v_ref[:, cols], ones_ext], axis=1)
            s_qk = lax.dot_general(q, k, (((1,), (1,)), ((), ())), preferred_element_type=F32)
            logw = jnp.where(mask, a_t[lane:lane + 1, :] - big_m[:, lane:lane + 1], -jnp.inf)
            s = (s_qk * jnp.exp(logw)).astype(MXU_DTYPE)
            c_old = c_ref[lane]
            r = _mm(s, v_ext) + w_inter[:, lane:lane + 1] * _mm(q, c_old.astype(MXU_DTYPE))
            den = jnp.maximum(jnp.abs(r[:, HEAD_DIM:]), exp_neg_m[:, lane:lane + 1])
            inv = 1.0 / den
            h_ref[:, cols] = r[:, :HEAD_DIM] * jnp.concatenate([inv] * (HEAD_DIM // LANES), axis=1)
            kw = (k.astype(F32) * w_state[:, lane:lane + 1]).astype(MXU_DTYPE)
            upd = lax.dot_general(kw, v_ext, (((0,), (0,)), ((), ())), preferred_element_type=F32)
            c_ref[lane] = decay[:, lane:lane + 1] * c_old + upd

    lane_idx = lax.broadcasted_iota(jnp.int32, (1, LANES), 1)
    m_ref[0:1, :] = jnp.where(lane_idx < N_HEADS, m_new_rows[0], m_new_rows[1])


def _mlstm(q, k, v, gi, gf, batch, seq):
    n = q.shape[0]
    nc = seq // CHUNK
    fwd = lambda w: pl.BlockSpec((CHUNK, w), lambda b, c: (b * nc + c, 0))
    bwd = lambda w: pl.BlockSpec((CHUNK, w), lambda b, c: (b * nc + nc - 1 - c, 0))
    return pl.pallas_call(
        _mlstm_kernel,
        grid=(batch, nc),
        in_specs=[fwd(D_MLSTM), fwd(D_MLSTM), fwd(D_MLSTM), fwd(LANES), fwd(LANES),
                  bwd(D_MLSTM), bwd(D_MLSTM), bwd(D_MLSTM), bwd(LANES), bwd(LANES)],
        out_specs=(fwd(D_MLSTM), bwd(D_MLSTM)),
        out_shape=(jax.ShapeDtypeStruct((n, D_MLSTM), F32), jax.ShapeDtypeStruct((n, D_MLSTM), F32)),
        scratch_shapes=[pltpu.VMEM((2 * N_HEADS, HEAD_DIM, HEAD_DIM + LANES), F32),
                        pltpu.VMEM((SUBLANES, LANES), F32)],
        compiler_params=pltpu.CompilerParams(
            dimension_semantics=("parallel", "arbitrary"), vmem_limit_bytes=VMEM_LIMIT),
        name="mlstm_scan",
    )(q, k, v, gi, gf, q, k, v, gi, gf)


def _outproj_kernel(hf_ref, hb_ref, o_ref, u_ref, up_ref, un_ref, x_ref,
                    mhg_ref, wp_ref, ps_ref, wo_ref, bo_ref, g_ref, b_ref,
                    y_ref, uext_ref, psum_ref, *, tiles_per_seq, seq, alpha):
    tm = x_ref.shape[0]
    t_in_seq = pl.program_id(0) % tiles_per_seq
    not_first = (t_in_seq > 0).astype(F32)
    not_last = (t_in_seq < tiles_per_seq - 1).astype(F32)

    uext_ref[0:HALO, :] = up_ref[...] * not_first
    uext_ref[HALO:HALO + tm, :] = u_ref[...]
    uext_ref[HALO + tm:, :] = un_ref[...] * not_last

    for r0 in range(0, tm, ROW_CHUNK):
        rows = slice(r0, r0 + ROW_CHUNK)
        hs = hf_ref[rows, :] + hb_ref[rows, :]
        parts = []
        for h in range(N_HEADS):
            xh = hs[:, h * HEAD_DIM:(h + 1) * HEAD_DIM]
            mu = jnp.mean(xh, axis=-1, keepdims=True)
            d = xh - mu
            var = jnp.mean(d * d, axis=-1, keepdims=True)
            parts.append(d * lax.rsqrt(var + LN_EPS))
        hm = jnp.concatenate(parts, axis=1) * mhg_ref[...] * jax.nn.sigmoid(o_ref[rows, :])

        pos = t_in_seq * tm + r0 + lax.broadcasted_iota(jnp.int32, (ROW_CHUNK, 1), 0)
        base = HALO + r0
        hp_parts = []
        for gi, w in enumerate(POOL_WINDOWS):
            cols = slice(gi * POOL_GROUP_DIM, (gi + 1) * POOL_GROUP_DIM)
            lo = base - w // 2
            n_rows = ROW_CHUNK + w - 2
            acc = uext_ref[lo:lo + n_rows, cols] + uext_ref[lo + 1:lo + 1 + n_rows, cols]
            span = 2
            while span < w:
                psum_ref[0:n_rows, cols] = acc
                n_rows -= span
                acc = psum_ref[0:n_rows, cols] + psum_ref[span:span + n_rows, cols]
                span *= 2
            cnt = (jnp.minimum(pos + (w - w // 2), seq) - jnp.maximum(pos - w // 2, 0)).astype(F32)
            pooled = acc / cnt - uext_ref[base:base + ROW_CHUNK, cols]
            hp_parts.append(_mm(pooled.astype(MXU_DTYPE), wp_ref[gi]))
        hp = jnp.concatenate(hp_parts, axis=1) * ps_ref[...]

        mixed = jnp.concatenate([hm, hp], axis=1).astype(MXU_DTYPE)
        mix = _mm(mixed, wo_ref[...]) + bo_ref[...]
        y_ref[rows, :] = _layer_norm_rows(alpha * x_ref[rows, :] + mix, g_ref[...], b_ref[...])


def _outproj(hf, hb, o, u, x2d, mh_g, w_pool, pool_scale, w_out, b_out, ln_g, ln_b, seq, tm, alpha):
    n = x2d.shape[0]
    tiles_per_seq = seq // tm
    hb_per_tile = tm // HALO
    n_halo_blocks = n // HALO
    row = lambda w: pl.BlockSpec((tm, w), lambda i: (i, 0))
    prev = pl.BlockSpec((HALO, D_POOL), lambda i: (jnp.maximum(i * hb_per_tile - 1, 0), 0))
    nxt = pl.BlockSpec((HALO, D_POOL),
                       lambda i: (jnp.minimum((i + 1) * hb_per_tile, n_halo_blocks - 1), 0))
    kern = functools.partial(_outproj_kernel, tiles_per_seq=tiles_per_seq, seq=seq, alpha=alpha)
    return pl.pallas_call(
        kern,
        grid=(n // tm,),
        in_specs=[row(D_MLSTM), row(D_MLSTM), row(D_MLSTM), row(D_POOL), prev, nxt, row(D_MODEL),
                  _const_spec((1, D_MLSTM)),
                  _const_spec((len(POOL_WINDOWS), POOL_GROUP_DIM, POOL_GROUP_DIM)),
                  _const_spec((1, D_POOL)), _const_spec((D_MODEL, D_MODEL)),
                  _const_spec((1, D_MODEL)), _const_spec((1, D_MODEL)), _const_spec((1, D_MODEL))],
        out_specs=row(D_MODEL),
        out_shape=jax.ShapeDtypeStruct((n, D_MODEL), F32),
        scratch_shapes=[pltpu.VMEM((tm + 2 * HALO, D_POOL), F32),
                        pltpu.VMEM((ROW_CHUNK + 2 * HALO, D_POOL), F32)],
        compiler_params=pltpu.CompilerParams(
            dimension_semantics=("parallel",), vmem_limit_bytes=VMEM_LIMIT),
        name="outproj_ln",
    )(hf, hb, o, u, u, u, x2d, mh_g, w_pool, pool_scale, w_out, b_out, ln_g, ln_b)


def _ffn_kernel(x_ref, xp_ref, xn_ref, wg_ref, wv_ref, vg_ref, vv_ref, wd_ref, bd_ref, g_ref, b_ref,
                y_ref, xs_ref, a_ref, f_ref, *, tiles_per_seq, alpha):
    tm = x_ref.shape[0]
    tf = wd_ref.shape[0]
    j = pl.program_id(1)
    t_in_seq = pl.program_id(0) % tiles_per_seq
    not_first = (t_in_seq > 0).astype(F32)
    not_last = (t_in_seq < tiles_per_seq - 1).astype(F32)

    @pl.when(j == 0)
    def _():
        x = x_ref[...]
        xs_ref[...] = jnp.concatenate([xp_ref[...], x, xn_ref[...]], axis=0).astype(xs_ref.dtype)
        y_ref[...] = alpha * x + bd_ref[...]

    xs = xs_ref[...]
    conv = []
    for part, (w_ref, vec_ref) in enumerate(((wg_ref, vg_ref), (wv_ref, vv_ref))):
        cols = slice(part * tf, (part + 1) * tf)
        a = _mm(xs, w_ref[...])
        outside = -vec_ref[0:1, :]
        a_ref[0:HALO, cols] = a[0:HALO] * not_first + outside * (1.0 - not_first)
        a_ref[HALO:HALO + tm, cols] = a[HALO:HALO + tm]
        a_ref[HALO + tm:, cols] = a[HALO + tm:] * not_last + outside * (1.0 - not_last)
    for r0 in range(0, tm, ROW_CHUNK):
        conv = []
        for part, vec_ref in enumerate((vg_ref, vv_ref)):
            cols = slice(part * tf, (part + 1) * tf)
            base = HALO + r0
            conv.append(vec_ref[1:2, :] * a_ref[base - 1:base - 1 + ROW_CHUNK, cols]
                        + vec_ref[2:3, :] * a_ref[base:base + ROW_CHUNK, cols]
                        + vec_ref[3:4, :] * a_ref[base + 1:base + 1 + ROW_CHUNK, cols]
                        + vec_ref[4:5, :])
        gate, val = conv
        f_ref[r0:r0 + ROW_CHUNK, :] = (gate * jax.nn.sigmoid(gate) * val).astype(MXU_DTYPE)
    f = f_ref[...]
    for cols in (slice(0, D_MODEL // 2), slice(D_MODEL // 2, D_MODEL)):
        y_ref[:, cols] += _mm(f, wd_ref[:, cols])

    @pl.when(j == pl.num_programs(1) - 1)
    def _():
        y_ref[...] = _layer_norm_rows(y_ref[...], g_ref[...], b_ref[...])


def _ffn(x1, w_up, ff_vecs, w_down, b_down, ln_g, ln_b, seq, tm, tf, alpha):
    n = x1.shape[0]
    tiles_per_seq = seq // tm
    hb_per_tile = tm // HALO
    n_halo_blocks = n // HALO
    nj = D_FF // tf
    main = pl.BlockSpec((tm, D_MODEL), lambda i, j: (i, 0))
    prev = pl.BlockSpec((HALO, D_MODEL), lambda i, j: (jnp.maximum(i * hb_per_tile - 1, 0), 0))
    nxt = pl.BlockSpec((HALO, D_MODEL),
                       lambda i, j: (jnp.minimum((i + 1) * hb_per_tile, n_halo_blocks - 1), 0))
    up_gate = pl.BlockSpec((D_MODEL, tf), lambda i, j: (0, j))
    up_val = pl.BlockSpec((D_MODEL, tf), lambda i, j: (0, j + nj))
    vec_gate = pl.BlockSpec((SUBLANES, tf), lambda i, j: (0, j))
    vec_val = pl.BlockSpec((SUBLANES, tf), lambda i, j: (0, j + nj))
    down = pl.BlockSpec((tf, D_MODEL), lambda i, j: (j, 0))
    vec = pl.BlockSpec((1, D_MODEL), lambda i, j: (0, 0))
    kern = functools.partial(_ffn_kernel, tiles_per_seq=tiles_per_seq, alpha=alpha)
    return pl.pallas_call(
        kern,
        grid=(n // tm, nj),
        in_specs=[main, prev, nxt, up_gate, up_val, vec_gate, vec_val, down, vec, vec, vec],
        out_specs=main,
        out_shape=jax.ShapeDtypeStruct((n, D_MODEL), F32),
        scratch_shapes=[pltpu.VMEM((tm + 2 * HALO, D_MODEL), MXU_DTYPE),
                        pltpu.VMEM((tm + 2 * HALO, 2 * tf), F32),
                        pltpu.VMEM((tm, tf), MXU_DTYPE)],
        compiler_params=pltpu.CompilerParams(
            dimension_semantics=("parallel", "arbitrary"), vmem_limit_bytes=VMEM_LIMIT),
        name="ffn_ln",
    )(x1, x1, x1, w_up, w_up, ff_vecs, ff_vecs, w_down, b_down, ln_g, ln_b)


def _pack_layer(w_in, b_in, mh_norm_g, w_pool, pool_scale, w_out, b_out, ln1_g, ln1_b,
                w_up, b_up, w_conv, b_conv, w_down, b_down, ln2_g, ln2_b):
    g0 = 4 * D_MLSTM
    h = N_HEADS
    ng = 4 * h

    def gate_cols(a):
        gts = a[..., g0:g0 + ng]
        pad = jnp.zeros(a.shape[:-1] + (LANES - 2 * h,), a.dtype)
        return jnp.concatenate([gts[..., 0:h], gts[..., 2 * h:3 * h], pad,
                                gts[..., h:2 * h], gts[..., 3 * h:4 * h], pad], axis=-1)

    pack_in = lambda a: jnp.concatenate([a[..., :g0], a[..., g0 + ng:], gate_cols(a)], axis=-1)
    row = lambda a: a.reshape(1, -1).astype(F32)

    return dict(
        w_qkvo=w_in[:, :g0].astype(MXU_DTYPE), w_u=w_in[:, g0 + ng:].astype(MXU_DTYPE),
        w_gates=gate_cols(w_in).astype(MXU_DTYPE), b_in=row(pack_in(b_in)),
        mh_g=row(mh_norm_g), w_pool=w_pool.astype(MXU_DTYPE), pool_scale=row(pool_scale),
        w_out=w_out.astype(MXU_DTYPE), b_out=row(b_out), ln1_g=row(ln1_g), ln1_b=row(ln1_b),
        w_up=w_up.astype(MXU_DTYPE),
        ff_vecs=jnp.concatenate(
            [b_up[None], w_conv, (b_conv + b_up * jnp.sum(w_conv, axis=0))[None],
             jnp.zeros((SUBLANES - 5, 2 * D_FF), F32)], axis=0).astype(F32),
        w_down=w_down.astype(MXU_DTYPE), b_down=row(b_down), ln2_g=row(ln2_g), ln2_b=row(ln2_b),
    )


def _tile(seq, want):
    return min(want, seq)


def _layer(x2d, p, batch, seq, alpha):
    q, k, v, o, u, gi, gf = _inproj(x2d, p["w_qkvo"], p["w_u"], p["w_gates"], p["b_in"],
                                    _tile(seq, 512))
    hf, hb = _mlstm(q, k, v, gi, gf, batch, seq)
    x1 = _outproj(hf, hb, o, u, x2d, p["mh_g"], p["w_pool"], p["pool_scale"], p["w_out"], p["b_out"],
                  p["ln1_g"], p["ln1_b"], seq, _tile(seq, 512), alpha)
    return _ffn(x1, p["w_up"], p["ff_vecs"], p["w_down"], p["b_down"], p["ln2_g"], p["ln2_b"],
                seq, _tile(seq, 512), FF_TILE, alpha)


def kernel(x_prompt, x_sample, w_in, b_in, mh_norm_g, w_pool, pool_scale, w_out, b_out, ln1_g, ln1_b,
           w_up, b_up, w_conv, b_conv, w_down, b_down, ln2_g, ln2_b):
    weights = (w_in, b_in, mh_norm_g, w_pool, pool_scale, w_out, b_out, ln1_g, ln1_b,
               w_up, b_up, w_conv, b_conv, w_down, b_down, ln2_g, ln2_b)
    depth = w_in.shape[0]
    alpha = (2.0 * depth) ** 0.25
    layers = [_pack_layer(*(w[l] for w in weights)) for l in range(depth)]

    def trunk(x):
        batch, seq, _ = x.shape
        x2d = x.reshape(batch * seq, D_MODEL)
        for p in layers:
            x2d = _layer(x2d, p, batch, seq, alpha)
        return x2d.reshape(batch, seq, D_MODEL)

    return trunk(x_prompt), trunk(x_sample)
```
